```python
import jax
import jax.numpy as jnp
from jax import lax
import numpy as np

D_MODEL = 4096
BATCH = 8
SEQ = 2048
DEPTH = 4
DEC_BATCH = 4
DEC_SEQ = 4096
PAST_LEN = 128

HEAD_DIM = 128
N_HEADS = D_MODEL // HEAD_DIM
N_MIXERS = 3
N_A_LAYERS = (DEPTH + 2) // 3
N_B_LAYERS = (DEPTH + 1) // 3
N_C_LAYERS = DEPTH // 3
GRID_W = 64
NA_ROWS = 8
NA_COLS = 16
NA_QCOLS = 16
NA_KCOLS = NA_QCOLS + NA_COLS
DIL_PATTERNS = ((128, 1), (512, 4), (2048, 16))
DIL_BLOCK = 64
C_KV_HEADS = N_HEADS // 4
C_WINDOW = 128
C_BLOCK = 128
N_GROUPS = 4
EXPERTS_PER_GROUP = 8
N_EXPERTS = N_GROUPS * EXPERTS_PER_GROUP
EXPERT_TOP_K = 2
D_EXPERT = D_MODEL // 4
MOE_BLOCK = 128
D_PLE = 256
NORM_EPS = 1e-6
NEG_INF = -1e30

kernel_name = 'hybrid_natten_dilated_swa_hmoe_encoder'


def rmsnorm(x, g):
    xf = x.astype(jnp.float32)
    y = xf * lax.rsqrt(jnp.mean(xf * xf, axis=-1, keepdims=True) + NORM_EPS)
    return (y * g.astype(jnp.float32)).astype(x.dtype)


def alibi_slopes(n):
    return 2.0 ** (-8.0 * jnp.arange(1, n + 1, dtype=jnp.float32) / n)


def neighbourhood_attention(q, k, v, rpb):
    b, s, h, d = q.shape
    rows = s // GRID_W
    wr = min(NA_ROWS, rows)
    n_cb = GRID_W // NA_QCOLS
    qg = q.reshape(b, rows, GRID_W, h, d)
    kg = k.reshape(b, rows, GRID_W, h, d)
    vg = v.reshape(b, rows, GRID_W, h, d)
    qcol = jnp.arange(GRID_W).reshape(n_cb, NA_QCOLS)
    kc0 = jnp.clip(jnp.arange(n_cb) * NA_QCOLS - NA_COLS // 2, 0, GRID_W - NA_KCOLS)
    kcol = kc0[:, None] + jnp.arange(NA_KCOLS)[None, :]
    cstart = jnp.clip(qcol - NA_COLS // 2, 0, GRID_W - NA_COLS)
    col_ok = (kcol[:, None, :] >= cstart[:, :, None]) & (kcol[:, None, :] < cstart[:, :, None] + NA_COLS)
    dcol = jnp.clip(kcol[:, None, :] - qcol[:, :, None] + NA_COLS - 1, 0, 2 * NA_COLS - 2)
    scale = d ** -0.5

    def row_step(r):
        rs = jnp.clip(r - wr // 2, 0, rows - wr)
        k_blk = lax.dynamic_slice_in_dim(kg, rs, wr, axis=1)[:, :, kcol]
        v_blk = lax.dynamic_slice_in_dim(vg, rs, wr, axis=1)[:, :, kcol]
        q_blk = lax.dynamic_index_in_dim(qg, r, axis=1, keepdims=False).reshape(b, n_cb, NA_QCOLS, h, d)
        drow = rs + jnp.arange(wr) - r + NA_ROWS - 1
        bias = rpb[:, drow[:, None, None, None], dcol[None]]
        bias = bias.astype(jnp.float32).transpose(2, 0, 3, 1, 4)
        sc = jnp.einsum('bnqhd,bwnkhd->bnhqwk', q_blk, k_blk).astype(jnp.float32) * scale
        sc = jnp.where(col_ok[None, :, None, :, None, :], sc + bias[None], NEG_INF)
        pr = jax.nn.softmax(sc.reshape(b, n_cb, h, NA_QCOLS, wr * NA_KCOLS), axis=-1).reshape(sc.shape)
        o = jnp.einsum('bnhqwk,bwnkhd->bnqhd', pr.astype(v.dtype), v_blk)
        return o.reshape(b, GRID_W, h, d)

    out = lax.map(row_step, jnp.arange(rows))
    return out.transpose(1, 0, 2, 3, 4).reshape(b, s, h, d)


def dilated_branch(q, k, v, slopes, dilation, half_span):
    b, s, h, d = q.shape
    L = s // dilation
    nb = -(-L // DIL_BLOCK)
    lp = nb * DIL_BLOCK

    def split(x):
        x = x.reshape(b, L, dilation, h, d).transpose(0, 2, 1, 3, 4)
        return jnp.pad(x, ((0, 0), (0, 0), (0, lp - L), (0, 0), (0, 0)))

    def band(x):
        xp = jnp.pad(x, ((0, 0), (0, 0), (DIL_BLOCK, DIL_BLOCK), (0, 0), (0, 0)))
        xp = xp.reshape(b, dilation, nb + 2, DIL_BLOCK, h, d)
        return jnp.concatenate([xp[:, :, :-2], xp[:, :, 1:-1], xp[:, :, 2:]], axis=3)

    qb = split(q).reshape(b, dilation, nb, DIL_BLOCK, h, d)
    kb = band(split(k))
    vb = band(split(v))
    uq = jnp.arange(nb)[:, None] * DIL_BLOCK + jnp.arange(DIL_BLOCK)[None, :]
    uk = jnp.arange(nb)[:, None] * DIL_BLOCK - DIL_BLOCK + jnp.arange(3 * DIL_BLOCK)[None, :]
    rel = jnp.abs(uk[:, None, :] - uq[:, :, None])
    valid = (rel <= half_span) & (uk[:, None, :] >= 0) & (uk[:, None, :] < L)
    dist = (rel * dilation).astype(jnp.float32)
    sc = jnp.einsum('brnqhd,brnkhd->brnhqk', qb, kb).astype(jnp.float32) * (d ** -0.5)
    sc = sc - slopes[:, None, None] * dist[:, None]
    sc = jnp.where(valid[:, None], sc, NEG_INF)
    m = jnp.max(sc, axis=-1, keepdims=True)
    e = jnp.exp(sc - m)
    l = jnp.sum(e, axis=-1, keepdims=True)
    o = jnp.einsum('brnhqk,brnkhd->brnqhd', (e / l).astype(v.dtype), vb).astype(jnp.float32)
    o = o.reshape(b, dilation, lp, h, d)[:, :, :L].transpose(0, 2, 1, 3, 4).reshape(b, s, h, d)

    def stat(z):
        z = z[..., 0].transpose(0, 1, 2, 4, 3).reshape(b, dilation, lp, h)[:, :, :L]
        return z.transpose(0, 2, 1, 3).reshape(b, s, h)

    return o, stat(m), stat(l)


def dilated_mixture(q, k, v, slopes):
    outs, ms, ls = [], [], []
    for window, dilation in DIL_PATTERNS:
        o, m, l = dilated_branch(q, k, v, slopes, dilation, window // (2 * dilation))
        outs.append(o)
        ms.append(m)
        ls.append(l)
    m_all = jnp.stack(ms)
    w = jnp.stack(ls) * jnp.exp(m_all - jnp.max(m_all, axis=0, keepdims=True))
    out = jnp.sum(w[..., None] * jnp.stack(outs), axis=0) / jnp.sum(w, axis=0)[..., None]
    return out.astype(q.dtype)


def windowed_gqa_sink(q, k, v, sink, slopes):
    b, s, h, d = q.shape
    kvh = k.shape[2]
    g = h // kvh
    nb = s // C_BLOCK
    qb = q.reshape(b, nb, C_BLOCK, kvh, g, d)

    def band(x):
        xp = jnp.pad(x, ((0, 0), (C_BLOCK, C_BLOCK), (0, 0), (0, 0))).reshape(b, nb + 2, C_BLOCK, kvh, d)
        return jnp.concatenate([xp[:, :-2], xp[:, 1:-1], xp[:, 2:]], axis=2)

    kb = band(k)
    vb = band(v)
    tq = jnp.arange(nb)[:, None] * C_BLOCK + jnp.arange(C_BLOCK)[None, :]
    tk = jnp.arange(nb)[:, None] * C_BLOCK - C_BLOCK + jnp.arange(3 * C_BLOCK)[None, :]
    rel = jnp.abs(tk[:, None, :] - tq[:, :, None])
    valid = (rel <= C_WINDOW) & (tk[:, None, :] >= 0) & (tk[:, None, :] < s)
    sc = jnp.einsum('bnqkgd,bnskd->bnkgqs', qb, kb).astype(jnp.float32) * (d ** -0.5)
    sc = sc - slopes.reshape(kvh, g)[:, :, None, None] * rel.astype(jnp.float32)[:, None, None]
    sc = jnp.where(valid[:, None, None], sc, NEG_INF)
    sink_col = jnp.broadcast_to(sink.astype(jnp.float32).reshape(kvh, g)[:, :, None, None], sc.shape[:-1] + (1,))
    pr = jax.nn.softmax(jnp.concatenate([sc, sink_col], axis=-1), axis=-1)[..., :-1]
    o = jnp.einsum('bnkgqs,bnskd->bnqkgd', pr.astype(v.dtype), vb)
    return o.reshape(b, s, h, d)


def hierarchical_moe(x, w_group, b_group, w_expert, b_expert, w_gate, w_up, w_down):
    t = x.shape[0]
    g_prob = jax.nn.softmax((x @ w_group).astype(jnp.float32) + b_group.astype(jnp.float32), axis=-1)
    g_p, g_idx = lax.top_k(g_prob, 1)
    e_logits = ((x @ w_expert).astype(jnp.float32) + b_expert.astype(jnp.float32)).reshape(t, N_GROUPS, EXPERTS_PER_GROUP)
    e_logits = e_logits[jnp.arange(t), g_idx[:, 0]]
    e_p, e_local = lax.top_k(jax.nn.softmax(e_logits, axis=-1), EXPERT_TOP_K)
    gates = g_p * (e_p / jnp.sum(e_p, axis=-1, keepdims=True))
    expert = g_idx * EXPERTS_PER_GROUP + e_local

    a = t * EXPERT_TOP_K
    flat_e = expert.reshape(a).astype(jnp.int32)
    flat_tok = jnp.arange(a, dtype=jnp.int32) // EXPERT_TOP_K
    flat_gate = gates.reshape(a)
    order = jnp.argsort(flat_e)
    sorted_e = flat_e[order]
    counts = jnp.bincount(flat_e, length=N_EXPERTS)
    padded = (counts + MOE_BLOCK - 1) // MOE_BLOCK * MOE_BLOCK
    start = jnp.cumsum(counts) - counts
    pend = jnp.cumsum(padded)
    pstart = pend - padded
    dest = pstart[sorted_e] + jnp.arange(a, dtype=jnp.int32) - start[sorted_e]
    n_blocks = -(-a // MOE_BLOCK) + N_EXPERTS
    cap = n_blocks * MOE_BLOCK
    slot_tok = jnp.full((cap,), t, jnp.int32).at[dest].set(flat_tok[order])
    slot_gate = jnp.zeros((cap,), jnp.float32).at[dest].set(flat_gate[order])
    block_e = jnp.clip(jnp.searchsorted(pend, jnp.arange(n_blocks) * MOE_BLOCK, side='right'), 0, N_EXPERTS - 1)
    x_pad = jnp.concatenate([x, jnp.zeros((1, x.shape[1]), x.dtype)], axis=0)
    xs = x_pad[slot_tok].reshape(n_blocks, MOE_BLOCK, x.shape[1])

    def expert_block(args):
        xb, e = args
        hb = jax.nn.silu(xb @ w_gate[e]) * (xb @ w_up[e])
        return hb @ w_down[e]

    ys = lax.map(expert_block, (xs, block_e)).reshape(cap, x.shape[1])
    y = jax.ops.segment_sum(ys * slot_gate[:, None].astype(ys.dtype), slot_tok, num_segments=t + 1)
    return y[:t]


def encoder_trunk(x, p, norm_mix, norm_ffn, a_w_qkv, a_rpb, a_w_o, b_w_qkv, b_w_o,
                  c_w_qkv, c_sink, c_w_o, moe_w_group, moe_b_group, moe_w_expert, moe_b_expert,
                  moe_w_gate, moe_w_up, moe_w_down, ple_w_proj, ple_norm, ple_w_gate, final_norm):
    b, s, dm = x.shape
    slopes = alibi_slopes(N_HEADS)
    h = x
    for i in range(DEPTH):
        kind = i % N_MIXERS
        j = i // N_MIXERS
        xn = rmsnorm(h, norm_mix[i])
        if kind == 0:
            q, k, v = jnp.split(xn @ a_w_qkv[j], 3, axis=-1)
            o = neighbourhood_attention(q.reshape(b, s, N_HEADS, HEAD_DIM), k.reshape(b, s, N_HEADS, HEAD_DIM),
                                        v.reshape(b, s, N_HEADS, HEAD_DIM), a_rpb[j])
            h = h + o.reshape(b, s, dm) @ a_w_o[j]
        elif kind == 1:
            q, k, v = jnp.split(xn @ b_w_qkv[j], 3, axis=-1)
            o = dilated_mixture(q.reshape(b, s, N_HEADS, HEAD_DIM), k.reshape(b, s, N_HEADS, HEAD_DIM),
                                v.reshape(b, s, N_HEADS, HEAD_DIM), slopes)
            h = h + o.reshape(b, s, dm) @ b_w_o[j]
        else:
            kvw = C_KV_HEADS * HEAD_DIM
            q, k, v = jnp.split(xn @ c_w_qkv[j], [D_MODEL, D_MODEL + kvw], axis=-1)
            o = windowed_gqa_sink(q.reshape(b, s, N_HEADS, HEAD_DIM), k.reshape(b, s, C_KV_HEADS, HEAD_DIM),
                                  v.reshape(b, s, C_KV_HEADS, HEAD_DIM), c_sink[j], slopes)
            h = h + o.reshape(b, s, dm) @ c_w_o[j]
        xn = rmsnorm(h, norm_ffn[i]).reshape(b * s, dm)
        h = h + hierarchical_moe(xn, moe_w_group[i], moe_b_group[i], moe_w_expert[i], moe_b_expert[i],
                                 moe_w_gate[i], moe_w_up[i], moe_w_down[i]).reshape(b, s, dm)
        gate = jax.nn.sigmoid(rmsnorm(h, ple_norm[i]) @ ple_w_gate[i])
        h = h + gate * (p[i] @ ple_w_proj[i])
    return rmsnorm(h, final_norm)


def setup_inputs(seed: int = 0) -> dict:
    key = jax.random.key(seed)
    ks = jax.random.split(key, 25)
    f32 = jnp.float32

    def nrm(k, shape, fan_in):
        return jax.random.normal(k, shape, f32) * (fan_in ** -0.5)

    def gain(k, shape):
        return 1.0 + 0.05 * jax.random.normal(k, shape, f32)

    c_qkv = D_MODEL + 2 * C_KV_HEADS * HEAD_DIM
    return {
        'x_prompt': jax.random.normal(ks[0], (BATCH, SEQ, D_MODEL), f32),
        'x_sample': jax.random.normal(ks[1], (DEC_BATCH, DEC_SEQ, D_MODEL), f32),
        'p_prompt': jax.random.normal(ks[2], (DEPTH, BATCH, SEQ, D_PLE), f32),
        'p_sample': jax.random.normal(ks[3], (DEPTH, DEC_BATCH, DEC_SEQ, D_PLE), f32),
        'norm_mix': gain(ks[4], (DEPTH, D_MODEL)),
        'norm_ffn': gain(ks[5], (DEPTH, D_MODEL)),
        'a_w_qkv': nrm(ks[6], (N_A_LAYERS, D_MODEL, 3 * D_MODEL), D_MODEL),
        'a_rpb': 0.1 * jax.random.normal(ks[7], (N_A_LAYERS, N_HEADS, 2 * NA_ROWS - 1, 2 * NA_COLS - 1), f32),
        'a_w_o': nrm(ks[8], (N_A_LAYERS, D_MODEL, D_MODEL), D_MODEL),
        'b_w_qkv': nrm(ks[9], (N_B_LAYERS, D_MODEL, 3 * D_MODEL), D_MODEL),
        'b_w_o': nrm(ks[10], (N_B_LAYERS, D_MODEL, D_MODEL), D_MODEL),
        'c_w_qkv': nrm(ks[11], (N_C_LAYERS, D_MODEL, c_qkv), D_MODEL),
        'c_sink': jax.random.normal(ks[12], (N_C_LAYERS, N_HEADS), f32),
        'c_w_o': nrm(ks[13], (N_C_LAYERS, D_MODEL, D_MODEL), D_MODEL),
        'moe_w_group': nrm(ks[14], (DEPTH, D_MODEL, N_GROUPS), D_MODEL),
        'moe_b_group': 0.01 * jax.random.normal(ks[15], (DEPTH, N_GROUPS), f32),
        'moe_w_expert': nrm(ks[16], (DEPTH, D_MODEL, N_EXPERTS), D_MODEL),
        'moe_b_expert': 0.01 * jax.random.normal(ks[17], (DEPTH, N_EXPERTS), f32),
        'moe_w_gate': nrm(ks[18], (DEPTH, N_EXPERTS, D_MODEL, D_EXPERT), D_MODEL),
        'moe_w_up': nrm(ks[19], (DEPTH, N_EXPERTS, D_MODEL, D_EXPERT), D_MODEL),
        'moe_w_down': nrm(ks[20], (DEPTH, N_EXPERTS, D_EXPERT, D_MODEL), D_EXPERT),
        'ple_w_proj': nrm(ks[21], (DEPTH, D_PLE, D_MODEL), D_PLE),
        'ple_norm': gain(ks[22], (DEPTH, D_MODEL)),
        'ple_w_gate': nrm(ks[23], (DEPTH, D_MODEL, D_MODEL), D_MODEL),
        'final_norm': gain(ks[24], (D_MODEL,)),
    }


def reference(x_prompt, x_sample, p_prompt, p_sample, norm_mix, norm_ffn, a_w_qkv, a_rpb, a_w_o,
              b_w_qkv, b_w_o, c_w_qkv, c_sink, c_w_o, moe_w_group, moe_b_group, moe_w_expert,
              moe_b_expert, moe_w_gate, moe_w_up, moe_w_down, ple_w_proj, ple_norm, ple_w_gate, final_norm):
    y_prompt = encoder_trunk(x_prompt, p_prompt, norm_mix, norm_ffn, a_w_qkv, a_rpb, a_w_o, b_w_qkv, b_w_o,
                             c_w_qkv, c_sink, c_w_o, moe_w_group, moe_b_group, moe_w_expert, moe_b_expert,
                             moe_w_gate, moe_w_up, moe_w_down, ple_w_proj, ple_norm, ple_w_gate, final_norm)
    y_sample = encoder_trunk(x_sample, p_sample, norm_mix, norm_ffn, a_w_qkv, a_rpb, a_w_o, b_w_qkv, b_w_o,
                             c_w_qkv, c_sink, c_w_o, moe_w_group, moe_b_group, moe_w_expert, moe_b_expert,
                             moe_w_gate, moe_w_up, moe_w_down, ple_w_proj, ple_norm, ple_w_gate, final_norm)
    return (y_prompt, y_sample)
```

```python
import functools

import jax
import jax.numpy as jnp
from jax import lax
from jax.experimental import pallas as pl
from jax.experimental.pallas import tpu as pltpu

D_MODEL = 4096
HEAD_DIM = 128
N_HEADS = D_MODEL // HEAD_DIM
DEPTH = 4
N_MIXERS = 3
GRID_W = 64
NA_ROWS = 8
NA_COLS = 16
DIL_PATTERNS = ((128, 1), (512, 4), (2048, 16))
C_KV_HEADS = N_HEADS // 4
C_GROUP = N_HEADS // C_KV_HEADS
C_WINDOW = 128
N_GROUPS = 4
EXPERTS_PER_GROUP = 8
N_EXPERTS = N_GROUPS * EXPERTS_PER_GROUP
EXPERT_TOP_K = 2
D_EXPERT = D_MODEL // 4
D_PLE = 256
NORM_EPS = 1e-6
NEG_INF = -1e30
SCALE = HEAD_DIM ** -0.5

LANES = 128
VMEM_LIMIT = 56 * 1024 * 1024
MOE_BLOCK = 256
F32 = jnp.float32
BF16 = jnp.bfloat16


def _params(*sem):
    return pltpu.CompilerParams(dimension_semantics=sem, vmem_limit_bytes=VMEM_LIMIT)


def _rmsnorm_kernel(x_ref, g_ref, o_ref):
    x = x_ref[...]
    ms = jnp.mean(x * x, axis=-1, keepdims=True)
    o_ref[...] = (x * lax.rsqrt(ms + NORM_EPS) * g_ref[...]).astype(o_ref.dtype)


def rmsnorm(x, g, out_dtype, tm=256):
    m, d = x.shape
    return pl.pallas_call(
        _rmsnorm_kernel,
        grid=(m // tm,),
        in_specs=[pl.BlockSpec((tm, d), lambda i: (i, 0)), pl.BlockSpec((1, d), lambda i: (0, 0))],
        out_specs=pl.BlockSpec((tm, d), lambda i: (i, 0)),
        out_shape=jax.ShapeDtypeStruct((m, d), out_dtype),
        compiler_params=_params("parallel"),
        name="rmsnorm",
    )(x, g.reshape(1, d))


def _mm_kernel(x_ref, w_ref, o_ref):
    o_ref[...] = jnp.dot(x_ref[...], w_ref[...], preferred_element_type=F32).astype(o_ref.dtype)


def matmul(x, w, out_dtype, tm=1024, tn=1024):
    m, k = x.shape
    n = w.shape[1]
    return pl.pallas_call(
        _mm_kernel,
        grid=(m // tm, n // tn),
        in_specs=[pl.BlockSpec((tm, k), lambda i, j: (i, 0)), pl.BlockSpec((k, tn), lambda i, j: (0, j))],
        out_specs=pl.BlockSpec((tm, tn), lambda i, j: (i, j)),
        out_shape=jax.ShapeDtypeStruct((m, n), out_dtype),
        compiler_params=_params("parallel", "parallel"),
        name="matmul",
    )(x, w)


def _mm_res_kernel(x_ref, w_ref, r_ref, o_ref):
    o_ref[...] = r_ref[...] + jnp.dot(x_ref[...], w_ref[...], preferred_element_type=F32)


def matmul_residual(x, w, res, tm=1024, tn=512):
    m, k = x.shape
    n = w.shape[1]
    return pl.pallas_call(
        _mm_res_kernel,
        grid=(m // tm, n // tn),
        in_specs=[
            pl.BlockSpec((tm, k), lambda i, j: (i, 0)),
            pl.BlockSpec((k, tn), lambda i, j: (0, j)),
            pl.BlockSpec((tm, tn), lambda i, j: (i, j)),
        ],
        out_specs=pl.BlockSpec((tm, tn), lambda i, j: (i, j)),
        out_shape=jax.ShapeDtypeStruct((m, n), F32),
        compiler_params=_params("parallel", "parallel"),
        name="matmul_residual",
    )(x, w, res)


def _ple_kernel(x_ref, wg_ref, p_ref, wp_ref, r_ref, o_ref):
    gate_logit = jnp.dot(x_ref[...], wg_ref[...], preferred_element_type=F32)
    proj = jnp.dot(p_ref[...], wp_ref[...], preferred_element_type=F32)
    gate = 1.0 / (1.0 + jnp.exp(-gate_logit))
    o_ref[...] = r_ref[...] + gate * proj


def ple_update(xn, w_gate, p, w_proj, res, tm=1024, tn=512):
    m, k = xn.shape
    n = w_gate.shape[1]
    kp = p.shape[1]
    return pl.pallas_call(
        _ple_kernel,
        grid=(m // tm, n // tn),
        in_specs=[
            pl.BlockSpec((tm, k), lambda i, j: (i, 0)),
            pl.BlockSpec((k, tn), lambda i, j: (0, j)),
            pl.BlockSpec((tm, kp), lambda i, j: (i, 0)),
            pl.BlockSpec((kp, tn), lambda i, j: (0, j)),
            pl.BlockSpec((tm, tn), lambda i, j: (i, j)),
        ],
        out_specs=pl.BlockSpec((tm, tn), lambda i, j: (i, j)),
        out_shape=jax.ShapeDtypeStruct((m, n), F32),
        compiler_params=_params("parallel", "parallel"),
        name="ple_update",
    )(xn, w_gate, p, w_proj, res)


def _qk_scores(q, k):
    return lax.dot_general(q, k, (((1,), (1,)), ((), ())), preferred_element_type=F32)


def _na_kernel(q_ref, k_ref, v_ref, b_ref, o_ref, *, rows, unroll):
    kw = NA_ROWS * GRID_W

    def body(it, carry):
        for u in range(unroll):
            r = it * unroll + u
            rs = jnp.clip(r - NA_ROWS // 2, 0, rows - NA_ROWS)
            delta = rs - r + NA_ROWS - 1
            q0 = pl.multiple_of(r * GRID_W, GRID_W)
            k0 = pl.multiple_of(rs * GRID_W, GRID_W)
            q = q_ref[pl.ds(q0, GRID_W), :]
            k = k_ref[pl.ds(k0, kw), :]
            v = v_ref[pl.ds(k0, kw), :]
            s = _qk_scores(q, k) * SCALE + b_ref[delta]
            m = jnp.max(s, axis=-1, keepdims=True)
            e = jnp.exp(s - m)
            l = jnp.sum(e, axis=-1, keepdims=True)
            o = jnp.dot(e.astype(BF16), v, preferred_element_type=F32) / l
            o_ref[pl.ds(q0, GRID_W), :] = o.astype(o_ref.dtype)
        return carry

    lax.fori_loop(0, rows // unroll, body, 0)


def na_bias_slabs(rpb):
    qc = jnp.arange(GRID_W)
    kc = jnp.arange(GRID_W)
    cstart = jnp.clip(qc - NA_COLS // 2, 0, GRID_W - NA_COLS)
    col_ok = (kc[None, :] >= cstart[:, None]) & (kc[None, :] < cstart[:, None] + NA_COLS)
    dcol = jnp.clip(kc[None, :] - qc[:, None] + NA_COLS - 1, 0, 2 * NA_COLS - 2)
    drow = jnp.arange(NA_ROWS)[:, None] + jnp.arange(NA_ROWS)[None, :]
    t = rpb[:, drow[:, :, None, None], dcol[None, None]]
    t = jnp.where(col_ok[None, None, None], t, NEG_INF)
    return t.transpose(0, 1, 3, 2, 4).reshape(rpb.shape[0], NA_ROWS, GRID_W, NA_ROWS * GRID_W)


def neighbourhood_attention(qkv, slabs, n_seq, seq_off, s):
    rows = s // GRID_W
    blk = lambda c0: pl.BlockSpec((s, HEAD_DIM), lambda b, h: (seq_off + b, c0 + h))
    return pl.pallas_call(
        functools.partial(_na_kernel, rows=rows, unroll=4),
        grid=(n_seq, N_HEADS),
        in_specs=[
            blk(0), blk(N_HEADS), blk(2 * N_HEADS),
            pl.BlockSpec((None, NA_ROWS, GRID_W, NA_ROWS * GRID_W), lambda b, h: (h, 0, 0, 0)),
        ],
        out_specs=pl.BlockSpec((s, HEAD_DIM), lambda b, h: (b, h)),
        out_shape=jax.ShapeDtypeStruct((n_seq * s, D_MODEL), BF16),
        compiler_params=_params("parallel", "parallel"),
        name="neighbourhood_attention",
    )(qkv, qkv, qkv, slabs)


def _dil_kernel(slope_ref, q_ref, k_ref, v_ref, o_ref, acc_ref, m_ref, l_ref, *, s):
    slope = slope_ref[pl.program_id(1)]
    for bi, (window, dil) in enumerate(DIL_PATTERNS):
        half = window // (2 * dil)
        sub = s // dil
        qb = min(sub, 256)
        kb = sub if sub <= 256 else qb + 2 * half
        nch = sub // qb
        stride = dil if dil > 1 else None

        def body(it, carry, bi=bi, dil=dil, half=half, sub=sub, qb=qb, kb=kb, nch=nch, stride=stride):
            res = it // nch
            u0 = (it % nch) * qb
            ks = jnp.clip(u0 - half, 0, sub - kb)
            q_idx = pl.ds(res + u0 * dil, qb, stride=stride)
            k_idx = pl.ds(res + ks * dil, kb, stride=stride)
            q = (q_ref[q_idx, :] * SCALE).astype(BF16)
            k = k_ref[k_idx, :].astype(BF16)
            v = v_ref[k_idx, :].astype(BF16)
            uq = u0 + lax.broadcasted_iota(jnp.int32, (qb, kb), 0)
            uk = ks + lax.broadcasted_iota(jnp.int32, (qb, kb), 1)
            rel = jnp.abs(uk - uq)
            sc = _qk_scores(q, k) - (slope * dil) * rel.astype(F32)
            sc = jnp.where(rel <= half, sc, NEG_INF)
            m_i = jnp.max(sc, axis=-1, keepdims=True)
            e = jnp.exp(sc - m_i)
            l_i = jnp.sum(e, axis=-1, keepdims=True)
            a_i = jnp.dot(e.astype(BF16), v, preferred_element_type=F32)
            m_b = jnp.broadcast_to(m_i, (qb, HEAD_DIM))
            l_b = jnp.broadcast_to(l_i, (qb, HEAD_DIM))
            if bi == 0:
                acc_ref[q_idx, :] = a_i
                m_ref[q_idx, :] = m_b
                l_ref[q_idx, :] = l_b
            else:
                m_o = m_ref[q_idx, :]
                m_n = jnp.maximum(m_o, m_b)
                w_o = jnp.exp(m_o - m_n)
                w_i = jnp.exp(m_b - m_n)
                acc_ref[q_idx, :] = w_o * acc_ref[q_idx, :] + w_i * a_i
                l_ref[q_idx, :] = w_o * l_ref[q_idx, :] + w_i * l_b
                m_ref[q_idx, :] = m_n
            return carry

        lax.fori_loop(0, dil * nch, body, 0)
    o_ref[...] = (acc_ref[...] / l_ref[...]).astype(o_ref.dtype)


def dilated_attention(qkv, slopes, n_seq, seq_off, s):
    blk = lambda c0: pl.BlockSpec((s, HEAD_DIM), lambda b, h: (seq_off + b, c0 + h))
    return pl.pallas_call(
        functools.partial(_dil_kernel, s=s),
        grid=(n_seq, N_HEADS),
        in_specs=[pl.BlockSpec(memory_space=pltpu.SMEM), blk(0), blk(N_HEADS), blk(2 * N_HEADS)],
        out_specs=pl.BlockSpec((s, HEAD_DIM), lambda b, h: (b, h)),
        out_shape=jax.ShapeDtypeStruct((n_seq * s, D_MODEL), BF16),
        scratch_shapes=[pltpu.VMEM((s, HEAD_DIM), F32)] * 3,
        compiler_params=_params("parallel", "parallel"),
        name="dilated_attention",
    )(slopes, qkv, qkv, qkv)


def _gqa_kernel(slope_ref, sink_ref, q_ref, k_ref, v_ref, o_ref, *, s, qb):
    kvh = pl.program_id(1)
    kb = qb + 2 * C_WINDOW

    def body(n, carry):
        q0 = pl.multiple_of(n * qb, qb)
        ks = pl.multiple_of(jnp.clip(n * qb - C_WINDOW, 0, s - kb), C_WINDOW)
        k = k_ref[pl.ds(ks, kb), :]
        v = v_ref[pl.ds(ks, kb), :]
        tq = q0 + lax.broadcasted_iota(jnp.int32, (qb, kb), 0)
        tk = ks + lax.broadcasted_iota(jnp.int32, (qb, kb), 1)
        rel = jnp.abs(tk - tq)
        valid = rel <= C_WINDOW
        relf = rel.astype(F32)
        for g in range(C_GROUP):
            head = kvh * C_GROUP + g
            slope = slope_ref[head]
            sink = sink_ref[head]
            q = q_ref[pl.ds(q0, qb), g * HEAD_DIM:(g + 1) * HEAD_DIM]
            sc = _qk_scores(q, k) * SCALE - slope * relf
            sc = jnp.where(valid, sc, NEG_INF)
            m = jnp.maximum(jnp.max(sc, axis=-1, keepdims=True), sink)
            e = jnp.exp(sc - m)
            l = jnp.sum(e, axis=-1, keepdims=True) + jnp.exp(sink - m)
            o = jnp.dot(e.astype(BF16), v, preferred_element_type=F32) / l
            o_ref[pl.ds(q0, qb), g * HEAD_DIM:(g + 1) * HEAD_DIM] = o.astype(o_ref.dtype)
        return carry

    lax.fori_loop(0, s // qb, body, 0)


def gqa_sink_attention(qkv, slopes, sink, n_seq, seq_off, s):
    gw = C_GROUP * HEAD_DIM
    kv0 = D_MODEL // HEAD_DIM
    smem = pl.BlockSpec(memory_space=pltpu.SMEM)
    return pl.pallas_call(
        functools.partial(_gqa_kernel, s=s, qb=256),
        grid=(n_seq, C_KV_HEADS),
        in_specs=[
            smem, smem,
            pl.BlockSpec((s, gw), lambda b, h: (seq_off + b, h)),
            pl.BlockSpec((s, HEAD_DIM), lambda b, h: (seq_off + b, kv0 + h)),
            pl.BlockSpec((s, HEAD_DIM), lambda b, h: (seq_off + b, kv0 + C_KV_HEADS + h)),
        ],
        out_specs=pl.BlockSpec((s, gw), lambda b, h: (b, h)),
        out_shape=jax.ShapeDtypeStruct((n_seq * s, D_MODEL), BF16),
        compiler_params=_params("parallel", "parallel"),
        name="gqa_sink_attention",
    )(slopes, sink, qkv, qkv, qkv)


def _route_kernel(h_ref, g_ref, w_ref, b_ref, xn_ref, r_ref):
    x = h_ref[...]
    ms = jnp.mean(x * x, axis=-1, keepdims=True)
    xn = x * lax.rsqrt(ms + NORM_EPS) * g_ref[...]
    xn_ref[...] = xn
    logits = jnp.dot(xn, w_ref[...], precision=lax.Precision.HIGHEST, preferred_element_type=F32) + b_ref[...]
    lane = lax.broadcasted_iota(jnp.int32, logits.shape, 1).astype(F32)
    rmax = lambda a: jnp.max(a, axis=-1, keepdims=True)
    rmin = lambda a: jnp.min(a, axis=-1, keepdims=True)
    rsum = lambda a: jnp.sum(a, axis=-1, keepdims=True)

    gmask = lane < N_GROUPS
    gl = jnp.where(gmask, logits, NEG_INF)
    ge = jnp.where(gmask, jnp.exp(gl - rmax(gl)), 0.0)
    gp = jnp.where(gmask, ge / rsum(ge), -1.0)
    g_p = rmax(gp)
    g_idx = rmin(jnp.where(gp == g_p, lane, float(LANES)))

    lo = N_GROUPS + EXPERTS_PER_GROUP * g_idx
    emask = (lane >= lo) & (lane < lo + EXPERTS_PER_GROUP)
    el = jnp.where(emask, logits, NEG_INF)
    ee = jnp.where(emask, jnp.exp(el - rmax(el)), 0.0)
    ep = jnp.where(emask, ee / rsum(ee), -1.0)
    p1 = rmax(ep)
    i1 = rmin(jnp.where(ep == p1, lane, float(LANES)))
    ep2 = jnp.where(lane == i1, -1.0, ep)
    p2 = rmax(ep2)
    i2 = rmin(jnp.where(ep2 == p2, lane, float(LANES)))
    den = p1 + p2
    out = jnp.where(lane == 0, g_p * (p1 / den), 0.0)
    out = jnp.where(lane == 1, g_p * (p2 / den), out)
    out = jnp.where(lane == 2, i1 - N_GROUPS, out)
    out = jnp.where(lane == 3, i2 - N_GROUPS, out)
    r_ref[...] = out


def moe_route(h, g, w_group, b_group, w_expert, b_expert, tm=256):
    t, d = h.shape
    n_log = N_GROUPS + N_EXPERTS
    w = jnp.zeros((d, LANES), F32).at[:, :n_log].set(jnp.concatenate([w_group, w_expert], axis=1))
    b = jnp.zeros((1, LANES), F32).at[0, :n_log].set(jnp.concatenate([b_group, b_expert]))
    return pl.pallas_call(
        _route_kernel,
        grid=(t // tm,),
        in_specs=[
            pl.BlockSpec((tm, d), lambda i: (i, 0)),
            pl.BlockSpec((1, d), lambda i: (0, 0)),
            pl.BlockSpec((d, LANES), lambda i: (0, 0)),
            pl.BlockSpec((1, LANES), lambda i: (0, 0)),
        ],
        out_specs=[pl.BlockSpec((tm, d), lambda i: (i, 0)), pl.BlockSpec((tm, LANES), lambda i: (i, 0))],
        out_shape=[jax.ShapeDtypeStruct((t, d), F32), jax.ShapeDtypeStruct((t, LANES), F32)],
        compiler_params=_params("parallel"),
        name="moe_route",
    )(h, g.reshape(1, d), w, b)


def moe_plan(expert_ids):
    t = expert_ids.shape[0]
    a = t * EXPERT_TOP_K
    n_blocks = a // MOE_BLOCK + N_EXPERTS
    cap = n_blocks * MOE_BLOCK
    flat_e = expert_ids.reshape(a)
    order = jnp.argsort(flat_e, stable=True).astype(jnp.int32)
    sorted_e = flat_e[order]
    counts = jnp.sum((flat_e[:, None] == jnp.arange(N_EXPERTS, dtype=jnp.int32)[None, :]).astype(jnp.int32), axis=0)
    padded = (counts + MOE_BLOCK - 1) // MOE_BLOCK * MOE_BLOCK
    start = jnp.cumsum(counts) - counts
    pend = jnp.cumsum(padded)
    pstart = pend - padded
    dest = (pstart[sorted_e] + jnp.arange(a, dtype=jnp.int32) - start[sorted_e]).astype(jnp.int32)
    slot_tok = jnp.zeros((cap,), jnp.int32).at[dest].set(order // EXPERT_TOP_K)
    pos = jnp.zeros((a,), jnp.int32).at[order].set(dest)
    block_e = jnp.clip(
        jnp.searchsorted(pend, jnp.arange(n_blocks, dtype=jnp.int32) * MOE_BLOCK, side="right"), 0, N_EXPERTS - 1
    ).astype(jnp.int32)
    return slot_tok, pos, block_e


def _row_copy(src_ref, dst_ref, sem, src_row, dst_row):
    return pltpu.make_async_copy(src_ref.at[pl.ds(src_row, 1), :], dst_ref.at[pl.ds(dst_row, 1), :], sem)


def _gather_kernel(idx_ref, src_ref, o_ref, buf, sem, *, tm):
    base = pl.program_id(0) * tm

    def issue(r, carry):
        _row_copy(src_ref, buf, sem, idx_ref[base + r], r).start()
        return carry

    lax.fori_loop(0, tm, issue, 0)
    pltpu.make_async_copy(src_ref.at[pl.ds(0, tm), :], buf, sem).wait()
    o_ref[...] = buf[...].astype(o_ref.dtype)


def gather_rows(src, idx, out_dtype, tm=MOE_BLOCK):
    n = idx.shape[0]
    d = src.shape[1]
    return pl.pallas_call(
        functools.partial(_gather_kernel, tm=tm),
        grid_spec=pltpu.PrefetchScalarGridSpec(
            num_scalar_prefetch=1,
            grid=(n // tm,),
            in_specs=[pl.BlockSpec(memory_space=pl.ANY)],
            out_specs=pl.BlockSpec((tm, d), lambda i, idx: (i, 0)),
            scratch_shapes=[pltpu.VMEM((tm, d), F32), pltpu.SemaphoreType.DMA(())],
        ),
        out_shape=jax.ShapeDtypeStruct((n, d), out_dtype),
        compiler_params=_params("arbitrary"),
        name="gather_rows",
    )(idx, src)


def _moe_up_kernel(be_ref, x_ref, wg_ref, wu_ref, o_ref):
    x = x_ref[...]
    g = jnp.dot(x, wg_ref[...], preferred_element_type=F32)
    u = jnp.dot(x, wu_ref[...], preferred_element_type=F32)
    o_ref[...] = (g * (1.0 / (1.0 + jnp.exp(-g))) * u).astype(o_ref.dtype)


def moe_up(xs, w_gate, w_up, block_e, tn=512):
    cap, d = xs.shape
    n = w_gate.shape[2]
    w_spec = pl.BlockSpec((None, d, tn), lambda j, rb, be: (be[rb], 0, j))
    return pl.pallas_call(
        _moe_up_kernel,
        grid_spec=pltpu.PrefetchScalarGridSpec(
            num_scalar_prefetch=1,
            grid=(n // tn, cap // MOE_BLOCK),
            in_specs=[pl.BlockSpec((MOE_BLOCK, d), lambda j, rb, be: (rb, 0)), w_spec, w_spec],
            out_specs=pl.BlockSpec((MOE_BLOCK, tn), lambda j, rb, be: (rb, j)),
        ),
        out_shape=jax.ShapeDtypeStruct((cap, n), BF16),
        compiler_params=_params("arbitrary", "arbitrary"),
        name="moe_up",
    )(block_e, xs, w_gate, w_up)


def _moe_down_kernel(be_ref, x_ref, w_ref, o_ref):
    o_ref[...] = jnp.dot(x_ref[...], w_ref[...], preferred_element_type=F32)


def moe_down(hs, w_down, block_e):
    cap, k = hs.shape
    n = w_down.shape[2]
    return pl.pallas_call(
        _moe_down_kernel,
        grid_spec=pltpu.PrefetchScalarGridSpec(
            num_scalar_prefetch=1,
            grid=(cap // MOE_BLOCK,),
            in_specs=[
                pl.BlockSpec((MOE_BLOCK, k), lambda rb, be: (rb, 0)),
                pl.BlockSpec((None, k, n), lambda rb, be: (be[rb], 0, 0)),
            ],
            out_specs=pl.BlockSpec((MOE_BLOCK, n), lambda rb, be: (rb, 0)),
        ),
        out_shape=jax.ShapeDtypeStruct((cap, n), F32),
        compiler_params=_params("arbitrary"),
        name="moe_down",
    )(block_e, hs, w_down)


def _combine_kernel(pos_ref, h_ref, g_ref, y_ref, o_ref, buf0, buf1, sems, *, tm):
    base = pl.program_id(0) * tm

    def issue(r, carry):
        a = (base + r) * EXPERT_TOP_K
        _row_copy(y_ref, buf0, sems.at[0], pos_ref[a], r).start()
        _row_copy(y_ref, buf1, sems.at[1], pos_ref[a + 1], r).start()
        return carry

    lax.fori_loop(0, tm, issue, 0)
    pltpu.make_async_copy(y_ref.at[pl.ds(0, tm), :], buf0, sems.at[0]).wait()
    pltpu.make_async_copy(y_ref.at[pl.ds(0, tm), :], buf1, sems.at[1]).wait()
    g = g_ref[...]
    o_ref[...] = h_ref[...] + (g[:, 0:1] * buf0[...] + g[:, 1:2] * buf1[...])


def moe_combine(h, route, ys, pos, tm=256):
    t, d = h.shape
    return pl.pallas_call(
        functools.partial(_combine_kernel, tm=tm),
        grid_spec=pltpu.PrefetchScalarGridSpec(
            num_scalar_prefetch=1,
            grid=(t // tm,),
            in_specs=[
                pl.BlockSpec((tm, d), lambda i, pos: (i, 0)),
                pl.BlockSpec((tm, LANES), lambda i, pos: (i, 0)),
                pl.BlockSpec(memory_space=pl.ANY),
            ],
            out_specs=pl.BlockSpec((tm, d), lambda i, pos: (i, 0)),
            scratch_shapes=[pltpu.VMEM((tm, d), F32), pltpu.VMEM((tm, d), F32), pltpu.SemaphoreType.DMA((2,))],
        ),
        out_shape=jax.ShapeDtypeStruct((t, d), F32),
        compiler_params=_params("arbitrary"),
        name="moe_combine",
    )(pos, h, route, ys)


def hierarchical_moe_update(h, g, w_group, b_group, w_expert, b_expert, w_gate, w_up, w_down):
    xn, route = moe_route(h, g, w_group, b_group, w_expert, b_expert)
    expert_ids = route[:, 2:4].astype(jnp.int32)
    slot_tok, pos, block_e = moe_plan(expert_ids)
    xs = gather_rows(xn, slot_tok, BF16)
    hs = moe_up(xs, w_gate, w_up, block_e)
    ys = moe_down(hs, w_down, block_e)
    return moe_combine(h, route, ys, pos)


def kernel(x_prompt, x_sample, p_prompt, p_sample, norm_mix, norm_ffn, a_w_qkv, a_rpb, a_w_o, b_w_qkv, b_w_o,
           c_w_qkv, c_sink, c_w_o, moe_w_group, moe_b_group, moe_w_expert, moe_b_expert, moe_w_gate, moe_w_up,
           moe_w_down, ple_w_proj, ple_norm, ple_w_gate, final_norm):
    b1, s1, d = x_prompt.shape
    b2, s2, _ = x_sample.shape
    t1 = b1 * s1
    t2 = b2 * s2
    assert t1 % s2 == 0
    seqs = ((b1, 0, s1), (b2, t1 // s2, s2))

    h = jnp.concatenate([x_prompt.reshape(t1, d), x_sample.reshape(t2, d)], axis=0)
    p = jnp.concatenate([p_prompt.reshape(DEPTH, t1, D_PLE), p_sample.reshape(DEPTH, t2, D_PLE)], axis=1).astype(BF16)
    slopes = 2.0 ** (-8.0 * jnp.arange(1, N_HEADS + 1, dtype=F32) / N_HEADS)

    for i in range(DEPTH):
        kind = i % N_MIXERS
        j = i // N_MIXERS
        xn = rmsnorm(h, norm_mix[i], BF16)
        if kind == 0:
            qkv = matmul(xn, a_w_qkv[j].astype(BF16), BF16)
            slabs = na_bias_slabs(a_rpb[j])
            o = [neighbourhood_attention(qkv, slabs, n, off, s) for n, off, s in seqs]
            w_o = a_w_o[j]
        elif kind == 1:
            qkv = matmul(xn, b_w_qkv[j].astype(BF16), F32)
            o = [dilated_attention(qkv, slopes, n, off, s) for n, off, s in seqs]
            w_o = b_w_o[j]
        else:
            qkv = matmul(xn, c_w_qkv[j].astype(BF16), BF16)
            o = [gqa_sink_attention(qkv, slopes, c_sink[j], n, off, s) for n, off, s in seqs]
            w_o = c_w_o[j]
        h = matmul_residual(jnp.concatenate(o, axis=0), w_o.astype(BF16), h)
        h = hierarchical_moe_update(
            h, norm_ffn[i], moe_w_group[i], moe_b_group[i], moe_w_expert[i], moe_b_expert[i],
            moe_w_gate[i].astype(BF16), moe_w_up[i].astype(BF16), moe_w_down[i].astype(BF16))
        xn = rmsnorm(h, ple_norm[i], BF16)
        h = ple_update(xn, ple_w_gate[i].astype(BF16), p[i], ple_w_proj[i].astype(BF16), h)

    y = rmsnorm(h, final_norm, F32)
    return y[:t1].reshape(b1, s1, d), y[t1:].reshape(b2, s2, d)
```

```python
import functools

import jax
import jax.numpy as jnp
from jax import lax
from jax.experimental import pallas as pl
from jax.experimental.pallas import tpu as pltpu

D_MODEL = 4096
HEAD_DIM = 128
N_HEADS = D_MODEL // HEAD_DIM
DEPTH = 4
N_MIXERS = 3
GRID_W = 64
NA_ROWS = 8
NA_COLS = 16
DIL_PATTERNS = ((128, 1), (512, 4), (2048, 16))
C_KV_HEADS = N_HEADS // 4
C_GROUP = N_HEADS // C_KV_HEADS
C_WINDOW = 128
N_GROUPS = 4
EXPERTS_PER_GROUP = 8
N_EXPERTS = N_GROUPS * EXPERTS_PER_GROUP
EXPERT_TOP_K = 2
D_EXPERT = D_MODEL // 4
D_PLE = 256
NORM_EPS = 1e-6
NEG_INF = -1e30
SCALE = HEAD_DIM ** -0.5

LANES = 128
VMEM_LIMIT = 56 * 1024 * 1024
MOE_BLOCK = 256
F32 = jnp.float32
BF16 = jnp.bfloat16


def _params(*sem):
    return pltpu.CompilerParams(dimension_semantics=sem, vmem_limit_bytes=VMEM_LIMIT)


def _rmsnorm_kernel(x_ref, g_ref, o_ref):
    x = x_ref[...]
    ms = jnp.mean(x * x, axis=-1, keepdims=True)
    o_ref[...] = (x * lax.rsqrt(ms + NORM_EPS) * g_ref[...]).astype(o_ref.dtype)


def rmsnorm(x, g, out_dtype, row0=0, m=None, tm=256):
    d = x.shape[1]
    m = x.shape[0] if m is None else m
    b0 = row0 // tm
    return pl.pallas_call(
        _rmsnorm_kernel,
        grid=(m // tm,),
        in_specs=[pl.BlockSpec((tm, d), lambda i: (b0 + i, 0)), pl.BlockSpec((1, d), lambda i: (0, 0))],
        out_specs=pl.BlockSpec((tm, d), lambda i: (i, 0)),
        out_shape=jax.ShapeDtypeStruct((m, d), out_dtype),
        compiler_params=_params("parallel"),
        name="rmsnorm",
    )(x, g.reshape(1, d))


def _mm_kernel(x_ref, w_ref, o_ref):
    o_ref[...] = jnp.dot(x_ref[...], w_ref[...], preferred_element_type=F32).astype(o_ref.dtype)


def matmul(x, w, out_dtype, tm=1024, tn=1024):
    m, k = x.shape
    n = w.shape[1]
    return pl.pallas_call(
        _mm_kernel,
        grid=(m // tm, n // tn),
        in_specs=[pl.BlockSpec((tm, k), lambda i, j: (i, 0)), pl.BlockSpec((k, tn), lambda i, j: (0, j))],
        out_specs=pl.BlockSpec((tm, tn), lambda i, j: (i, j)),
        out_shape=jax.ShapeDtypeStruct((m, n), out_dtype),
        compiler_params=_params("parallel", "parallel"),
        name="matmul",
    )(x, w)


def _mm_res_kernel(xa_ref, xb_ref, w_ref, r_ref, o_ref, *, na):
    @pl.when(pl.program_id(0) < na)
    def _():
        o_ref[...] = r_ref[...] + jnp.dot(xa_ref[...], w_ref[...], preferred_element_type=F32)

    @pl.when(pl.program_id(0) >= na)
    def _():
        o_ref[...] = r_ref[...] + jnp.dot(xb_ref[...], w_ref[...], preferred_element_type=F32)


def matmul_residual(xa, xb, w, res, tm=512, tn=1024):
    ma, k = xa.shape
    mb = xb.shape[0]
    n = w.shape[1]
    na = ma // tm
    return pl.pallas_call(
        functools.partial(_mm_res_kernel, na=na),
        grid=((ma + mb) // tm, n // tn),
        in_specs=[
            pl.BlockSpec((tm, k), lambda i, j: (jnp.minimum(i, na - 1), 0)),
            pl.BlockSpec((tm, k), lambda i, j: (jnp.maximum(i - na, 0), 0)),
            pl.BlockSpec((k, tn), lambda i, j: (0, j)),
            pl.BlockSpec((tm, tn), lambda i, j: (i, j)),
        ],
        out_specs=pl.BlockSpec((tm, tn), lambda i, j: (i, j)),
        out_shape=jax.ShapeDtypeStruct((ma + mb, n), F32),
        compiler_params=_params("arbitrary", "arbitrary"),
        name="matmul_residual",
    )(xa, xb, w, res)


def _ple_kernel(x_ref, wg_ref, p_ref, wp_ref, r_ref, o_ref):
    gate_logit = jnp.dot(x_ref[...], wg_ref[...], preferred_element_type=F32)
    proj = jnp.dot(p_ref[...], wp_ref[...], preferred_element_type=F32)
    gate = 1.0 / (1.0 + jnp.exp(-gate_logit))
    o_ref[...] = r_ref[...] + gate * proj


def ple_update(xn, w_gate, p, w_proj, res, tm=1024, tn=512):
    m, k = xn.shape
    n = w_gate.shape[1]
    kp = p.shape[1]
    return pl.pallas_call(
        _ple_kernel,
        grid=(m // tm, n // tn),
        in_specs=[
            pl.BlockSpec((tm, k), lambda i, j: (i, 0)),
            pl.BlockSpec((k, tn), lambda i, j: (0, j)),
            pl.BlockSpec((tm, kp), lambda i, j: (i, 0)),
            pl.BlockSpec((kp, tn), lambda i, j: (0, j)),
            pl.BlockSpec((tm, tn), lambda i, j: (i, j)),
        ],
        out_specs=pl.BlockSpec((tm, tn), lambda i, j: (i, j)),
        out_shape=jax.ShapeDtypeStruct((m, n), F32),
        compiler_params=_params("parallel", "parallel"),
        name="ple_update",
    )(xn, w_gate, p, w_proj, res)


def _qk_scores(q, k):
    return lax.dot_general(q, k, (((1,), (1,)), ((), ())), preferred_element_type=F32)


NA_QROWS = 4
NA_KROWS = NA_QROWS + NA_ROWS
NA_VARIANTS = 3


def _na_kernel(q_ref, k_ref, v_ref, b_ref, o_ref, *, rows, unroll):
    nq = NA_QROWS * GRID_W
    nk = NA_KROWS * GRID_W
    nblk = rows // NA_QROWS

    def body(it, carry):
        for u in range(unroll):
            blk = it * unroll + u
            r0 = blk * NA_QROWS
            ks = jnp.clip(r0 - NA_ROWS // 2, 0, rows - NA_KROWS)
            variant = jnp.where(blk == 0, 0, jnp.where(blk == nblk - 1, 2, 1))
            q0 = pl.multiple_of(r0 * GRID_W, nq)
            k0 = pl.multiple_of(ks * GRID_W, GRID_W)
            q = q_ref[pl.ds(q0, nq), :]
            k = k_ref[pl.ds(k0, nk), :]
            v = v_ref[pl.ds(k0, nk), :]
            s = _qk_scores(q, k) * SCALE + b_ref[variant]
            m = jnp.max(s, axis=-1, keepdims=True)
            e = jnp.exp(s - m)
            l = jnp.sum(e, axis=-1, keepdims=True)
            o = jnp.dot(e.astype(BF16), v, preferred_element_type=F32) / l
            o_ref[pl.ds(q0, nq), :] = o.astype(o_ref.dtype)
        return carry

    lax.fori_loop(0, nblk // unroll, body, 0)


def _na_slab_tables():
    import numpy as np
    rowsel = np.zeros((NA_VARIANTS, NA_QROWS, NA_KROWS, 2 * NA_ROWS - 1), np.float32)
    half = NA_ROWS // 2
    for v in range(NA_VARIANTS):
        for i in range(NA_QROWS):
            k_first = (0, -half, NA_QROWS - NA_KROWS)[v]
            w_first = (0, i - half, NA_QROWS - NA_ROWS)[v]
            for a in range(NA_KROWS):
                if 0 <= k_first + a - w_first < NA_ROWS:
                    rowsel[v, i, a, k_first + a - i + NA_ROWS - 1] = 1.0
    colsel = np.zeros((GRID_W, GRID_W, 2 * NA_COLS - 1), np.float32)
    for qc in range(GRID_W):
        cstart = min(max(qc - NA_COLS // 2, 0), GRID_W - NA_COLS)
        for kc in range(cstart, cstart + NA_COLS):
            colsel[qc, kc, kc - qc + NA_COLS - 1] = 1.0
    return rowsel, colsel


def na_bias_slabs(rpb):
    rowsel, colsel = _na_slab_tables()
    hi = lax.Precision.HIGHEST
    t = jnp.einsum("hrc,viar->hviac", rpb, rowsel, precision=hi)
    t = jnp.einsum("hviac,qkc->hviqak", t, colsel, precision=hi)
    ok = (rowsel.sum(-1)[:, :, None, :, None] * colsel.sum(-1)[None, None, :, None, :]) > 0
    t = jnp.where(ok[None], t, NEG_INF)
    return t.reshape(rpb.shape[0], NA_VARIANTS, NA_QROWS * GRID_W, NA_KROWS * GRID_W)


def neighbourhood_attention(qkv, slabs, n_seq, seq_off, s):
    rows = s // GRID_W
    assert rows % (2 * NA_QROWS) == 0 and rows >= NA_KROWS + NA_QROWS
    blk = lambda c0: pl.BlockSpec((s, HEAD_DIM), lambda b, h: (seq_off + b, c0 + h))
    return pl.pallas_call(
        functools.partial(_na_kernel, rows=rows, unroll=2),
        grid=(n_seq, N_HEADS),
        in_specs=[
            blk(0), blk(N_HEADS), blk(2 * N_HEADS),
            pl.BlockSpec((None, NA_VARIANTS, NA_QROWS * GRID_W, NA_KROWS * GRID_W), lambda b, h: (h, 0, 0, 0)),
        ],
        out_specs=pl.BlockSpec((s, HEAD_DIM), lambda b, h: (b, h)),
        out_shape=jax.ShapeDtypeStruct((n_seq * s, D_MODEL), BF16),
        compiler_params=_params("parallel", "parallel"),
        name="neighbourhood_attention",
    )(qkv, qkv, qkv, slabs)


def _dil_kernel(slope_ref, q_ref, k_ref, v_ref, o_ref, acc_ref, m_ref, l_ref, *, s):
    slope = slope_ref[pl.program_id(1)]
    for bi, (window, dil) in enumerate(DIL_PATTERNS):
        half = window // (2 * dil)
        sub = s // dil
        qb = min(sub, 256)
        kb = sub if sub <= 256 else qb + 2 * half
        nch = sub // qb
        stride = dil if dil > 1 else None

        def chunk(it, dil=dil, half=half, sub=sub, qb=qb, kb=kb, nch=nch, stride=stride):
            res = it // nch
            u0 = (it % nch) * qb
            ks = jnp.clip(u0 - half, 0, sub - kb)
            q_idx = pl.ds(res + u0 * dil, qb, stride=stride)
            k_idx = pl.ds(res + ks * dil, kb, stride=stride)
            q = (q_ref[q_idx, :] * SCALE).astype(BF16)
            k = k_ref[k_idx, :].astype(BF16)
            v = v_ref[k_idx, :].astype(BF16)
            uq = u0 + lax.broadcasted_iota(jnp.int32, (qb, kb), 0)
            uk = ks + lax.broadcasted_iota(jnp.int32, (qb, kb), 1)
            rel = jnp.abs(uk - uq)
            sc = _qk_scores(q, k) - (slope * dil) * rel.astype(F32)
            sc = jnp.where(rel <= half, sc, NEG_INF)
            m_i = jnp.max(sc, axis=-1, keepdims=True)
            e = jnp.exp(sc - m_i)
            l_i = jnp.sum(e, axis=-1, keepdims=True)
            a_i = jnp.dot(e.astype(BF16), v, preferred_element_type=F32)
            return q_idx, a_i, jnp.broadcast_to(m_i, (qb, HEAD_DIM)), jnp.broadcast_to(l_i, (qb, HEAD_DIM))

        unroll = 4 if qb * kb <= 128 * 128 else 2

        def body(it, carry, bi=bi, chunk=chunk, unroll=unroll):
            for q_idx, a_i, m_b, l_b in [chunk(it * unroll + u) for u in range(unroll)]:
                if bi == 0:
                    acc_ref[q_idx, :] = a_i
                    m_ref[q_idx, :] = m_b
                    l_ref[q_idx, :] = l_b
                else:
                    m_o = m_ref[q_idx, :]
                    m_n = jnp.maximum(m_o, m_b)
                    w_o = jnp.exp(m_o - m_n)
                    w_i = jnp.exp(m_b - m_n)
                    acc_ref[q_idx, :] = w_o * acc_ref[q_idx, :] + w_i * a_i
                    l_ref[q_idx, :] = w_o * l_ref[q_idx, :] + w_i * l_b
                    m_ref[q_idx, :] = m_n
            return carry

        assert (dil * nch) % unroll == 0
        lax.fori_loop(0, dil * nch // unroll, body, 0)
    o_ref[...] = (acc_ref[...] / l_ref[...]).astype(o_ref.dtype)


def dilated_attention(qkv, slopes, n_seq, seq_off, s):
    blk = lambda c0: pl.BlockSpec((s, HEAD_DIM), lambda b, h: (seq_off + b, c0 + h))
    return pl.pallas_call(
        functools.partial(_dil_kernel, s=s),
        grid=(n_seq, N_HEADS),
        in_specs=[pl.BlockSpec(memory_space=pltpu.SMEM), blk(0), blk(N_HEADS), blk(2 * N_HEADS)],
        out_specs=pl.BlockSpec((s, HEAD_DIM), lambda b, h: (b, h)),
        out_shape=jax.ShapeDtypeStruct((n_seq * s, D_MODEL), BF16),
        scratch_shapes=[pltpu.VMEM((s, HEAD_DIM), F32)] * 3,
        compiler_params=_params("parallel", "parallel"),
        name="dilated_attention",
    )(slopes, qkv, qkv, qkv)


def _gqa_kernel(slope_ref, sink_ref, q_ref, k_ref, v_ref, o_ref, *, s, qb):
    kvh = pl.program_id(1)
    kb = qb + 2 * C_WINDOW

    def body(n, carry):
        q0 = pl.multiple_of(n * qb, qb)
        ks = pl.multiple_of(jnp.clip(n * qb - C_WINDOW, 0, s - kb), C_WINDOW)
        k = k_ref[pl.ds(ks, kb), :]
        v = v_ref[pl.ds(ks, kb), :]
        tq = q0 + lax.broadcasted_iota(jnp.int32, (qb, kb), 0)
        tk = ks + lax.broadcasted_iota(jnp.int32, (qb, kb), 1)
        rel = jnp.abs(tk - tq)
        valid = rel <= C_WINDOW
        relf = rel.astype(F32)
        for g in range(C_GROUP):
            head = kvh * C_GROUP + g
            slope = slope_ref[head]
            sink = sink_ref[head]
            q = q_ref[pl.ds(q0, qb), g * HEAD_DIM:(g + 1) * HEAD_DIM]
            sc = _qk_scores(q, k) * SCALE - slope * relf
            sc = jnp.where(valid, sc, NEG_INF)
            m = jnp.maximum(jnp.max(sc, axis=-1, keepdims=True), sink)
            e = jnp.exp(sc - m)
            l = jnp.sum(e, axis=-1, keepdims=True) + jnp.exp(sink - m)
            o = jnp.dot(e.astype(BF16), v, preferred_element_type=F32) / l
            o_ref[pl.ds(q0, qb), g * HEAD_DIM:(g + 1) * HEAD_DIM] = o.astype(o_ref.dtype)
        return carry

    lax.fori_loop(0, s // qb, body, 0)


def gqa_sink_attention(qkv, slopes, sink, n_seq, seq_off, s):
    gw = C_GROUP * HEAD_DIM
    kv0 = D_MODEL // HEAD_DIM
    smem = pl.BlockSpec(memory_space=pltpu.SMEM)
    return pl.pallas_call(
        functools.partial(_gqa_kernel, s=s, qb=256),
        grid=(n_seq, C_KV_HEADS),
        in_specs=[
            smem, smem,
            pl.BlockSpec((s, gw), lambda b, h: (seq_off + b, h)),
            pl.BlockSpec((s, HEAD_DIM), lambda b, h: (seq_off + b, kv0 + h)),
            pl.BlockSpec((s, HEAD_DIM), lambda b, h: (seq_off + b, kv0 + C_KV_HEADS + h)),
        ],
        out_specs=pl.BlockSpec((s, gw), lambda b, h: (b, h)),
        out_shape=jax.ShapeDtypeStruct((n_seq * s, D_MODEL), BF16),
        compiler_params=_params("parallel", "parallel"),
        name="gqa_sink_attention",
    )(slopes, sink, qkv, qkv, qkv)


def _route_kernel(h_ref, g_ref, w_ref, b_ref, xn_ref, r_ref):
    x = h_ref[...]
    ms = jnp.mean(x * x, axis=-1, keepdims=True)
    xn = x * lax.rsqrt(ms + NORM_EPS) * g_ref[...]
    xn_ref[...] = xn
    logits = jnp.dot(xn, w_ref[...], precision=lax.Precision.HIGHEST, preferred_element_type=F32) + b_ref[...]
    lane = lax.broadcasted_iota(jnp.int32, logits.shape, 1).astype(F32)
    rmax = lambda a: jnp.max(a, axis=-1, keepdims=True)
    rmin = lambda a: jnp.min(a, axis=-1, keepdims=True)
    rsum = lambda a: jnp.sum(a, axis=-1, keepdims=True)

    gmask = lane < N_GROUPS
    gl = jnp.where(gmask, logits, NEG_INF)
    ge = jnp.where(gmask, jnp.exp(gl - rmax(gl)), 0.0)
    gp = jnp.where(gmask, ge / rsum(ge), -1.0)
    g_p = rmax(gp)
    g_idx = rmin(jnp.where(gp == g_p, lane, float(LANES)))

    lo = N_GROUPS + EXPERTS_PER_GROUP * g_idx
    emask = (lane >= lo) & (lane < lo + EXPERTS_PER_GROUP)
    el = jnp.where(emask, logits, NEG_INF)
    ee = jnp.where(emask, jnp.exp(el - rmax(el)), 0.0)
    ep = jnp.where(emask, ee / rsum(ee), -1.0)
    p1 = rmax(ep)
    i1 = rmin(jnp.where(ep == p1, lane, float(LANES)))
    ep2 = jnp.where(lane == i1, -1.0, ep)
    p2 = rmax(ep2)
    i2 = rmin(jnp.where(ep2 == p2, lane, float(LANES)))
    den = p1 + p2
    out = jnp.where(lane == 0, g_p * (p1 / den), 0.0)
    out = jnp.where(lane == 1, g_p * (p2 / den), out)
    out = jnp.where(lane == 2, i1 - N_GROUPS, out)
    out = jnp.where(lane == 3, i2 - N_GROUPS, out)
    r_ref[...] = out


def moe_route(h, g, w_group, b_group, w_expert, b_expert, tm=256):
    t, d = h.shape
    n_log = N_GROUPS + N_EXPERTS
    w = jnp.zeros((d, LANES), F32).at[:, :n_log].set(jnp.concatenate([w_group, w_expert], axis=1))
    b = jnp.zeros((1, LANES), F32).at[0, :n_log].set(jnp.concatenate([b_group, b_expert]))
    return pl.pallas_call(
        _route_kernel,
        grid=(t // tm,),
        in_specs=[
            pl.BlockSpec((tm, d), lambda i: (i, 0)),
            pl.BlockSpec((1, d), lambda i: (0, 0)),
            pl.BlockSpec((d, LANES), lambda i: (0, 0)),
            pl.BlockSpec((1, LANES), lambda i: (0, 0)),
        ],
        out_specs=[pl.BlockSpec((tm, d), lambda i: (i, 0)), pl.BlockSpec((tm, LANES), lambda i: (i, 0))],
        out_shape=[jax.ShapeDtypeStruct((t, d), F32), jax.ShapeDtypeStruct((t, LANES), F32)],
        compiler_params=_params("parallel"),
        name="moe_route",
    )(h, g.reshape(1, d), w, b)


def moe_plan(expert_ids):
    t = expert_ids.shape[0]
    a = t * EXPERT_TOP_K
    n_blocks = a // MOE_BLOCK + N_EXPERTS
    cap = n_blocks * MOE_BLOCK
    flat_e = expert_ids.reshape(a)
    order = jnp.argsort(flat_e, stable=True).astype(jnp.int32)
    rank = jnp.argsort(order).astype(jnp.int32)
    counts = jnp.sum((flat_e[:, None] == jnp.arange(N_EXPERTS, dtype=jnp.int32)[None, :]).astype(jnp.int32), axis=0)
    padded = (counts + MOE_BLOCK - 1) // MOE_BLOCK * MOE_BLOCK
    start = jnp.cumsum(counts) - counts
    pend = jnp.cumsum(padded)
    pstart = pend - padded
    pos = (pstart[flat_e] + rank - start[flat_e]).astype(jnp.int32)
    block_first = jnp.arange(n_blocks, dtype=jnp.int32) * MOE_BLOCK
    block_e = jnp.minimum(jnp.sum((pend[None, :] <= block_first[:, None]).astype(jnp.int32), axis=1), N_EXPERTS - 1)
    slot = jnp.arange(cap, dtype=jnp.int32)
    slot_e = jnp.repeat(block_e, MOE_BLOCK)
    slot_rank = slot - pstart[slot_e]
    sorted_pos = jnp.clip(start[slot_e] + slot_rank, 0, a - 1)
    slot_tok = jnp.where(slot_rank < counts[slot_e], order[sorted_pos] // EXPERT_TOP_K, 0).astype(jnp.int32)
    return slot_tok, pos, block_e.astype(jnp.int32)


def _row_copy(src_ref, dst_ref, sem, src_row, dst_row):
    return pltpu.make_async_copy(src_ref.at[pl.ds(src_row, 1), :], dst_ref.at[pl.ds(dst_row, 1), :], sem)


GATHER_UNROLL = 8


def _start_row_gather(src_hbm, idx_ref, idx0, idx_stride, dst, sem):
    n = dst.shape[0]

    def issue(it, carry):
        for u in range(GATHER_UNROLL):
            r = it * GATHER_UNROLL + u
            _row_copy(src_hbm, dst, sem, idx_ref[idx0 + r * idx_stride], r).start()
        return carry

    lax.fori_loop(0, n // GATHER_UNROLL, issue, 0)


def _wait_row_gather(src_hbm, dst, sem):
    pltpu.make_async_copy(src_hbm.at[pl.ds(0, dst.shape[0]), :], dst, sem).wait()


def _moe_up_kernel(tok_ref, be_ref, x_hbm, wg_ref, wu_ref, o_ref, buf, sems):
    rb = pl.program_id(0)

    def start(block, slot):
        _start_row_gather(x_hbm, tok_ref, block * MOE_BLOCK, 1, buf.at[slot], sems.at[slot])

    @pl.when(rb == 0)
    def _():
        start(0, 0)

    @pl.when(rb + 1 < pl.num_programs(0))
    def _():
        start(rb + 1, (rb + 1) % 2)

    slot = rb % 2
    _wait_row_gather(x_hbm, buf.at[slot], sems.at[slot])
    x = buf[slot].astype(BF16)
    g = jnp.dot(x, wg_ref[...], preferred_element_type=F32)
    u = jnp.dot(x, wu_ref[...], preferred_element_type=F32)
    o_ref[...] = (g * (1.0 / (1.0 + jnp.exp(-g))) * u).astype(o_ref.dtype)


def moe_up(xn, slot_tok, w_gate, w_up, block_e):
    d = xn.shape[1]
    n = w_gate.shape[2]
    cap = slot_tok.shape[0]
    w_spec = pl.BlockSpec((None, d, n), lambda rb, tok, be: (be[rb], 0, 0))
    return pl.pallas_call(
        _moe_up_kernel,
        grid_spec=pltpu.PrefetchScalarGridSpec(
            num_scalar_prefetch=2,
            grid=(cap // MOE_BLOCK,),
            in_specs=[pl.BlockSpec(memory_space=pl.ANY), w_spec, w_spec],
            out_specs=pl.BlockSpec((MOE_BLOCK, n), lambda rb, tok, be: (rb, 0)),
            scratch_shapes=[pltpu.VMEM((2, MOE_BLOCK, d), F32), pltpu.SemaphoreType.DMA((2,))],
        ),
        out_shape=jax.ShapeDtypeStruct((cap, n), BF16),
        compiler_params=_params("arbitrary"),
        name="moe_up",
    )(slot_tok, block_e, xn, w_gate, w_up)


def _moe_down_kernel(be_ref, x_ref, w_ref, o_ref):
    o_ref[...] = jnp.dot(x_ref[...], w_ref[...], preferred_element_type=F32)


def moe_down(hs, w_down, block_e):
    cap, k = hs.shape
    n = w_down.shape[2]
    return pl.pallas_call(
        _moe_down_kernel,
        grid_spec=pltpu.PrefetchScalarGridSpec(
            num_scalar_prefetch=1,
            grid=(cap // MOE_BLOCK,),
            in_specs=[
                pl.BlockSpec((MOE_BLOCK, k), lambda rb, be: (rb, 0)),
                pl.BlockSpec((None, k, n), lambda rb, be: (be[rb], 0, 0)),
            ],
            out_specs=pl.BlockSpec((MOE_BLOCK, n), lambda rb, be: (rb, 0)),
        ),
        out_shape=jax.ShapeDtypeStruct((cap, n), F32),
        compiler_params=_params("arbitrary"),
        name="moe_down",
    )(block_e, hs, w_down)


def _combine_kernel(pos_ref, h_ref, g_ref, gain_ref, y_hbm, o_ref, xn_ref, buf, sems, *, tm):
    i = pl.program_id(0)

    def start(block, slot):
        for k in range(EXPERT_TOP_K):
            _start_row_gather(y_hbm, pos_ref, block * tm * EXPERT_TOP_K + k, EXPERT_TOP_K,
                              buf.at[slot, k], sems.at[slot, k])

    @pl.when(i == 0)
    def _():
        start(0, 0)

    @pl.when(i + 1 < pl.num_programs(0))
    def _():
        start(i + 1, (i + 1) % 2)

    slot = i % 2
    for k in range(EXPERT_TOP_K):
        _wait_row_gather(y_hbm, buf.at[slot, k], sems.at[slot, k])
    g = g_ref[...]
    hn = h_ref[...] + (g[:, 0:1] * buf[slot, 0] + g[:, 1:2] * buf[slot, 1])
    o_ref[...] = hn
    ms = jnp.mean(hn * hn, axis=-1, keepdims=True)
    xn_ref[...] = (hn * lax.rsqrt(ms + NORM_EPS) * gain_ref[...]).astype(xn_ref.dtype)


def moe_combine(h, route, ys, pos, next_gain, tm=256):
    t, d = h.shape
    row = lambda i, pos: (i, 0)
    return pl.pallas_call(
        functools.partial(_combine_kernel, tm=tm),
        grid_spec=pltpu.PrefetchScalarGridSpec(
            num_scalar_prefetch=1,
            grid=(t // tm,),
            in_specs=[
                pl.BlockSpec((tm, d), row),
                pl.BlockSpec((tm, LANES), row),
                pl.BlockSpec((1, d), lambda i, pos: (0, 0)),
                pl.BlockSpec(memory_space=pl.ANY),
            ],
            out_specs=[pl.BlockSpec((tm, d), row), pl.BlockSpec((tm, d), row)],
            scratch_shapes=[pltpu.VMEM((2, EXPERT_TOP_K, tm, d), F32), pltpu.SemaphoreType.DMA((2, EXPERT_TOP_K))],
        ),
        out_shape=[jax.ShapeDtypeStruct((t, d), F32), jax.ShapeDtypeStruct((t, d), BF16)],
        compiler_params=_params("arbitrary"),
        name="moe_combine",
    )(pos, h, route, next_gain.reshape(1, d), ys)


def hierarchical_moe_update(h, g, w_group, b_group, w_expert, b_expert, w_gate, w_up, w_down, next_gain):
    xn, route = moe_route(h, g, w_group, b_group, w_expert, b_expert)
    expert_ids = route[:, 2:4].astype(jnp.int32)
    slot_tok, pos, block_e = moe_plan(expert_ids)
    hs = moe_up(xn, slot_tok, w_gate, w_up, block_e)
    ys = moe_down(hs, w_down, block_e)
    return moe_combine(h, route, ys, pos, next_gain)


def kernel(x_prompt, x_sample, p_prompt, p_sample, norm_mix, norm_ffn, a_w_qkv, a_rpb, a_w_o, b_w_qkv, b_w_o,
           c_w_qkv, c_sink, c_w_o, moe_w_group, moe_b_group, moe_w_expert, moe_b_expert, moe_w_gate, moe_w_up,
           moe_w_down, ple_w_proj, ple_norm, ple_w_gate, final_norm):
    b1, s1, d = x_prompt.shape
    b2, s2, _ = x_sample.shape
    t1 = b1 * s1
    t2 = b2 * s2
    assert t1 % s2 == 0
    seqs = ((b1, 0, s1), (b2, t1 // s2, s2))

    h = jnp.concatenate([x_prompt.reshape(t1, d), x_sample.reshape(t2, d)], axis=0)
    p = jnp.concatenate([p_prompt.reshape(DEPTH, t1, D_PLE), p_sample.reshape(DEPTH, t2, D_PLE)], axis=1).astype(BF16)
    slopes = 2.0 ** (-8.0 * jnp.arange(1, N_HEADS + 1, dtype=F32) / N_HEADS)

    for i in range(DEPTH):
        kind = i % N_MIXERS
        j = i // N_MIXERS
        xn = rmsnorm(h, norm_mix[i], BF16)
        if kind == 0:
            qkv = matmul(xn, a_w_qkv[j].astype(BF16), BF16)
            slabs = na_bias_slabs(a_rpb[j])
            o = [neighbourhood_attention(qkv, slabs, n, off, s) for n, off, s in seqs]
            w_o = a_w_o[j]
        elif kind == 1:
            qkv = matmul(xn, b_w_qkv[j].astype(BF16), F32)
            o = [dilated_attention(qkv, slopes, n, off, s) for n, off, s in seqs]
            w_o = b_w_o[j]
        else:
            qkv = matmul(xn, c_w_qkv[j].astype(BF16), BF16)
            o = [gqa_sink_attention(qkv, slopes, c_sink[j], n, off, s) for n, off, s in seqs]
            w_o = c_w_o[j]
        h = matmul_residual(o[0], o[1], w_o.astype(BF16), h)
        h, xn = hierarchical_moe_update(
            h, norm_ffn[i], moe_w_group[i], moe_b_group[i], moe_w_expert[i], moe_b_expert[i],
            moe_w_gate[i].astype(BF16), moe_w_up[i].astype(BF16), moe_w_down[i].astype(BF16), ple_norm[i])
        h = ple_update(xn, ple_w_gate[i].astype(BF16), p[i], ple_w_proj[i].astype(BF16), h)

    y1 = rmsnorm(h, final_norm, F32, 0, t1)
    y2 = rmsnorm(h, final_norm, F32, t1, t2)
    return y1.reshape(b1, s1, d), y2.reshape(b2, s2, d)
```

```python
import functools

import jax
import jax.numpy as jnp
from jax import lax
from jax.experimental import pallas as pl
from jax.experimental.pallas import tpu as pltpu

D_MODEL = 4096
HEAD_DIM = 128
N_HEADS = D_MODEL // HEAD_DIM
DEPTH = 4
N_MIXERS = 3
GRID_W = 64
NA_ROWS = 8
NA_COLS = 16
DIL_PATTERNS = ((128, 1), (512, 4), (2048, 16))
C_KV_HEADS = N_HEADS // 4
C_GROUP = N_HEADS // C_KV_HEADS
C_WINDOW = 128
N_GROUPS = 4
EXPERTS_PER_GROUP = 8
N_EXPERTS = N_GROUPS * EXPERTS_PER_GROUP
EXPERT_TOP_K = 2
D_EXPERT = D_MODEL // 4
D_PLE = 256
NORM_EPS = 1e-6
NEG_INF = -1e30
SCALE = HEAD_DIM ** -0.5

LANES = 128
VMEM_LIMIT = 56 * 1024 * 1024
MOE_BLOCK = 256
F32 = jnp.float32
BF16 = jnp.bfloat16


def _params(*sem):
    return pltpu.CompilerParams(dimension_semantics=sem, vmem_limit_bytes=VMEM_LIMIT)


def _rmsnorm_kernel(x_ref, g_ref, o_ref):
    x = x_ref[...]
    ms = jnp.mean(x * x, axis=-1, keepdims=True)
    o_ref[...] = (x * lax.rsqrt(ms + NORM_EPS) * g_ref[...]).astype(o_ref.dtype)


def rmsnorm(x, g, out_dtype, row0=0, m=None, tm=256):
    d = x.shape[1]
    m = x.shape[0] if m is None else m
    b0 = row0 // tm
    return pl.pallas_call(
        _rmsnorm_kernel,
        grid=(m // tm,),
        in_specs=[pl.BlockSpec((tm, d), lambda i: (b0 + i, 0)), pl.BlockSpec((1, d), lambda i: (0, 0))],
        out_specs=pl.BlockSpec((tm, d), lambda i: (i, 0)),
        out_shape=jax.ShapeDtypeStruct((m, d), out_dtype),
        compiler_params=_params("parallel"),
        name="rmsnorm",
    )(x, g.reshape(1, d))


def _mm_kernel(x_ref, w_ref, o_ref):
    o_ref[...] = jnp.dot(x_ref[...], w_ref[...], preferred_element_type=F32).astype(o_ref.dtype)


def matmul(x, w, out_dtype, tm=1024, tn=1024):
    m, k = x.shape
    n = w.shape[1]
    return pl.pallas_call(
        _mm_kernel,
        grid=(m // tm, n // tn),
        in_specs=[pl.BlockSpec((tm, k), lambda i, j: (i, 0)), pl.BlockSpec((k, tn), lambda i, j: (0, j))],
        out_specs=pl.BlockSpec((tm, tn), lambda i, j: (i, j)),
        out_shape=jax.ShapeDtypeStruct((m, n), out_dtype),
        compiler_params=_params("parallel", "parallel"),
        name="matmul",
    )(x, w)


def _mm_res_kernel(xa_ref, xb_ref, w_ref, r_ref, o_ref, *, na):
    @pl.when(pl.program_id(0) < na)
    def _():
        o_ref[...] = r_ref[...] + jnp.dot(xa_ref[...], w_ref[...], preferred_element_type=F32)

    @pl.when(pl.program_id(0) >= na)
    def _():
        o_ref[...] = r_ref[...] + jnp.dot(xb_ref[...], w_ref[...], preferred_element_type=F32)


def matmul_residual(xa, xb, w, res, tm=512, tn=1024):
    ma, k = xa.shape
    mb = xb.shape[0]
    n = w.shape[1]
    na = ma // tm
    return pl.pallas_call(
        functools.partial(_mm_res_kernel, na=na),
        grid=((ma + mb) // tm, n // tn),
        in_specs=[
            pl.BlockSpec((tm, k), lambda i, j: (jnp.minimum(i, na - 1), 0)),
            pl.BlockSpec((tm, k), lambda i, j: (jnp.maximum(i - na, 0), 0)),
            pl.BlockSpec((k, tn), lambda i, j: (0, j)),
            pl.BlockSpec((tm, tn), lambda i, j: (i, j)),
        ],
        out_specs=pl.BlockSpec((tm, tn), lambda i, j: (i, j)),
        out_shape=jax.ShapeDtypeStruct((ma + mb, n), F32),
        compiler_params=_params("arbitrary", "arbitrary"),
        name="matmul_residual",
    )(xa, xb, w, res)


def _ple_kernel(x_ref, wg_ref, p_ref, wp_ref, r_ref, o_ref):
    gate_logit = jnp.dot(x_ref[...], wg_ref[...], preferred_element_type=F32)
    proj = jnp.dot(p_ref[...], wp_ref[...], preferred_element_type=F32)
    gate = 1.0 / (1.0 + jnp.exp(-gate_logit))
    o_ref[...] = r_ref[...] + gate * proj


def ple_update(xn, w_gate, p, w_proj, res, tm=1024, tn=512):
    m, k = xn.shape
    n = w_gate.shape[1]
    kp = p.shape[1]
    return pl.pallas_call(
        _ple_kernel,
        grid=(m // tm, n // tn),
        in_specs=[
            pl.BlockSpec((tm, k), lambda i, j: (i, 0)),
            pl.BlockSpec((k, tn), lambda i, j: (0, j)),
            pl.BlockSpec((tm, kp), lambda i, j: (i, 0)),
            pl.BlockSpec((kp, tn), lambda i, j: (0, j)),
            pl.BlockSpec((tm, tn), lambda i, j: (i, j)),
        ],
        out_specs=pl.BlockSpec((tm, tn), lambda i, j: (i, j)),
        out_shape=jax.ShapeDtypeStruct((m, n), F32),
        compiler_params=_params("parallel", "parallel"),
        name="ple_update",
    )(xn, w_gate, p, w_proj, res)


def _qk_scores(q, k):
    return lax.dot_general(q, k, (((1,), (1,)), ((), ())), preferred_element_type=F32)


NA_QROWS = 4
NA_KROWS = NA_QROWS + NA_ROWS
NA_VARIANTS = 3


def _na_kernel(q_ref, k_ref, v_ref, b_ref, o_ref, *, rows, unroll):
    nq = NA_QROWS * GRID_W
    nk = NA_KROWS * GRID_W
    nblk = rows // NA_QROWS

    def body(it, carry):
        for u in range(unroll):
            blk = it * unroll + u
            r0 = blk * NA_QROWS
            ks = jnp.clip(r0 - NA_ROWS // 2, 0, rows - NA_KROWS)
            variant = jnp.where(blk == 0, 0, jnp.where(blk == nblk - 1, 2, 1))
            q0 = pl.multiple_of(r0 * GRID_W, nq)
            k0 = pl.multiple_of(ks * GRID_W, GRID_W)
            q = q_ref[pl.ds(q0, nq), :]
            k = k_ref[pl.ds(k0, nk), :]
            v = v_ref[pl.ds(k0, nk), :]
            s = _qk_scores(q, k) * SCALE + b_ref[variant]
            m = jnp.max(s, axis=-1, keepdims=True)
            e = jnp.exp(s - m)
            l = jnp.sum(e, axis=-1, keepdims=True)
            o = jnp.dot(e.astype(BF16), v, preferred_element_type=F32) / l
            o_ref[pl.ds(q0, nq), :] = o.astype(o_ref.dtype)
        return carry

    lax.fori_loop(0, nblk // unroll, body, 0)


def _na_slab_tables():
    import numpy as np
    rowsel = np.zeros((NA_VARIANTS, NA_QROWS, NA_KROWS, 2 * NA_ROWS - 1), np.float32)
    half = NA_ROWS // 2
    for v in range(NA_VARIANTS):
        for i in range(NA_QROWS):
            k_first = (0, -half, NA_QROWS - NA_KROWS)[v]
            w_first = (0, i - half, NA_QROWS - NA_ROWS)[v]
            for a in range(NA_KROWS):
                if 0 <= k_first + a - w_first < NA_ROWS:
                    rowsel[v, i, a, k_first + a - i + NA_ROWS - 1] = 1.0
    colsel = np.zeros((GRID_W, GRID_W, 2 * NA_COLS - 1), np.float32)
    for qc in range(GRID_W):
        cstart = min(max(qc - NA_COLS // 2, 0), GRID_W - NA_COLS)
        for kc in range(cstart, cstart + NA_COLS):
            colsel[qc, kc, kc - qc + NA_COLS - 1] = 1.0
    return rowsel, colsel


def na_bias_slabs(rpb):
    rowsel, colsel = _na_slab_tables()
    hi = lax.Precision.HIGHEST
    t = jnp.einsum("hrc,viar->hviac", rpb, rowsel, precision=hi)
    t = jnp.einsum("hviac,qkc->hviqak", t, colsel, precision=hi)
    ok = (rowsel.sum(-1)[:, :, None, :, None] * colsel.sum(-1)[None, None, :, None, :]) > 0
    t = jnp.where(ok[None], t, NEG_INF)
    return t.reshape(rpb.shape[0], NA_VARIANTS, NA_QROWS * GRID_W, NA_KROWS * GRID_W)


def neighbourhood_attention(qkv, slabs, n_seq, seq_off, s):
    rows = s // GRID_W
    assert rows % (2 * NA_QROWS) == 0 and rows >= NA_KROWS + NA_QROWS
    blk = lambda c0: pl.BlockSpec((s, HEAD_DIM), lambda b, h: (seq_off + b, c0 + h))
    return pl.pallas_call(
        functools.partial(_na_kernel, rows=rows, unroll=2),
        grid=(n_seq, N_HEADS),
        in_specs=[
            blk(0), blk(N_HEADS), blk(2 * N_HEADS),
            pl.BlockSpec((None, NA_VARIANTS, NA_QROWS * GRID_W, NA_KROWS * GRID_W), lambda b, h: (h, 0, 0, 0)),
        ],
        out_specs=pl.BlockSpec((s, HEAD_DIM), lambda b, h: (b, h)),
        out_shape=jax.ShapeDtypeStruct((n_seq * s, D_MODEL), BF16),
        compiler_params=_params("parallel", "parallel"),
        name="neighbourhood_attention",
    )(qkv, qkv, qkv, slabs)


def _dil_kernel(slope_ref, q_ref, k_ref, v_ref, o_ref, acc_ref, m_ref, l_ref, *, s):
    slope = slope_ref[pl.program_id(1)]
    for bi, (window, dil) in enumerate(sorted(DIL_PATTERNS, key=lambda p: -p[1])):
        half = window // (2 * dil)
        sub = s // dil
        qb = min(sub, 256)
        kb = sub if sub <= 256 else qb + 2 * half
        nch = sub // qb
        stride = dil if dil > 1 else None

        def chunk(it, dil=dil, half=half, sub=sub, qb=qb, kb=kb, nch=nch, stride=stride):
            res = it // nch
            u0 = (it % nch) * qb
            ks = jnp.clip(u0 - half, 0, sub - kb)
            q_idx = pl.ds(res + u0 * dil, qb, stride=stride)
            k_idx = pl.ds(res + ks * dil, kb, stride=stride)
            q = (q_ref[q_idx, :] * SCALE).astype(BF16)
            k = k_ref[k_idx, :].astype(BF16)
            v = v_ref[k_idx, :].astype(BF16)
            uq = u0 + lax.broadcasted_iota(jnp.int32, (qb, kb), 0)
            uk = ks + lax.broadcasted_iota(jnp.int32, (qb, kb), 1)
            rel = jnp.abs(uk - uq)
            sc = _qk_scores(q, k) - (slope * dil) * rel.astype(F32)
            sc = jnp.where(rel <= half, sc, NEG_INF)
            m_i = jnp.max(sc, axis=-1, keepdims=True)
            e = jnp.exp(sc - m_i)
            l_i = jnp.sum(e, axis=-1, keepdims=True)
            a_i = jnp.dot(e.astype(BF16), v, preferred_element_type=F32)
            return q_idx, a_i, jnp.broadcast_to(m_i, (qb, HEAD_DIM)), jnp.broadcast_to(l_i, (qb, HEAD_DIM))

        unroll = 4 if qb * kb <= 128 * 128 else 2

        def body(it, carry, bi=bi, chunk=chunk, unroll=unroll):
            for q_idx, a_i, m_b, l_b in [chunk(it * unroll + u) for u in range(unroll)]:
                if bi == 0:
                    acc_ref[q_idx, :] = a_i
                    m_ref[q_idx, :] = m_b
                    l_ref[q_idx, :] = l_b
                else:
                    m_o = m_ref[q_idx, :]
                    m_n = jnp.maximum(m_o, m_b)
                    w_o = jnp.exp(m_o - m_n)
                    w_i = jnp.exp(m_b - m_n)
                    acc_ref[q_idx, :] = w_o * acc_ref[q_idx, :] + w_i * a_i
                    l_ref[q_idx, :] = w_o * l_ref[q_idx, :] + w_i * l_b
                    m_ref[q_idx, :] = m_n
            return carry

        assert (dil * nch) % unroll == 0
        lax.fori_loop(0, dil * nch // unroll, body, 0)
    o_ref[...] = (acc_ref[...] / l_ref[...]).astype(o_ref.dtype)


def dilated_attention(qkv, slopes, n_seq, seq_off, s):
    blk = lambda c0: pl.BlockSpec((s, HEAD_DIM), lambda b, h: (seq_off + b, c0 + h))
    return pl.pallas_call(
        functools.partial(_dil_kernel, s=s),
        grid=(n_seq, N_HEADS),
        in_specs=[pl.BlockSpec(memory_space=pltpu.SMEM), blk(0), blk(N_HEADS), blk(2 * N_HEADS)],
        out_specs=pl.BlockSpec((s, HEAD_DIM), lambda b, h: (b, h)),
        out_shape=jax.ShapeDtypeStruct((n_seq * s, D_MODEL), BF16),
        scratch_shapes=[pltpu.VMEM((s, HEAD_DIM), F32)] * 3,
        compiler_params=_params("parallel", "parallel"),
        name="dilated_attention",
    )(slopes, qkv, qkv, qkv)


def _gqa_kernel(slope_ref, sink_ref, q_ref, k_ref, v_ref, o_ref, *, s, qb):
    kvh = pl.program_id(1)
    kb = qb + 2 * C_WINDOW

    def body(n, carry):
        q0 = pl.multiple_of(n * qb, qb)
        ks = pl.multiple_of(jnp.clip(n * qb - C_WINDOW, 0, s - kb), C_WINDOW)
        k = k_ref[pl.ds(ks, kb), :]
        v = v_ref[pl.ds(ks, kb), :]
        tq = q0 + lax.broadcasted_iota(jnp.int32, (qb, kb), 0)
        tk = ks + lax.broadcasted_iota(jnp.int32, (qb, kb), 1)
        rel = jnp.abs(tk - tq)
        valid = rel <= C_WINDOW
        relf = rel.astype(F32)
        for g in range(C_GROUP):
            head = kvh * C_GROUP + g
            slope = slope_ref[head]
            sink = sink_ref[head]
            q = q_ref[pl.ds(q0, qb), g * HEAD_DIM:(g + 1) * HEAD_DIM]
            sc = _qk_scores(q, k) * SCALE - slope * relf
            sc = jnp.where(valid, sc, NEG_INF)
            m = jnp.maximum(jnp.max(sc, axis=-1, keepdims=True), sink)
            e = jnp.exp(sc - m)
            l = jnp.sum(e, axis=-1, keepdims=True) + jnp.exp(sink - m)
            o = jnp.dot(e.astype(BF16), v, preferred_element_type=F32) / l
            o_ref[pl.ds(q0, qb), g * HEAD_DIM:(g + 1) * HEAD_DIM] = o.astype(o_ref.dtype)
        return carry

    lax.fori_loop(0, s // qb, body, 0)


def gqa_sink_attention(qkv, slopes, sink, n_seq, seq_off, s):
    gw = C_GROUP * HEAD_DIM
    kv0 = D_MODEL // HEAD_DIM
    smem = pl.BlockSpec(memory_space=pltpu.SMEM)
    return pl.pallas_call(
        functools.partial(_gqa_kernel, s=s, qb=256),
        grid=(n_seq, C_KV_HEADS),
        in_specs=[
            smem, smem,
            pl.BlockSpec((s, gw), lambda b, h: (seq_off + b, h)),
            pl.BlockSpec((s, HEAD_DIM), lambda b, h: (seq_off + b, kv0 + h)),
            pl.BlockSpec((s, HEAD_DIM), lambda b, h: (seq_off + b, kv0 + C_KV_HEADS + h)),
        ],
        out_specs=pl.BlockSpec((s, gw), lambda b, h: (b, h)),
        out_shape=jax.ShapeDtypeStruct((n_seq * s, D_MODEL), BF16),
        compiler_params=_params("parallel", "parallel"),
        name="gqa_sink_attention",
    )(slopes, sink, qkv, qkv, qkv)


def _pack_bf16_pairs(x):
    half = x.shape[1] // 2
    hi = lax.bitcast_convert_type(x[:, :half].astype(BF16).astype(F32), jnp.uint32)
    lo = lax.bitcast_convert_type(x[:, half:].astype(BF16).astype(F32), jnp.uint32)
    return hi | (lo >> 16)


def _unpack_bf16_pairs(w):
    hi = lax.bitcast_convert_type(w & jnp.uint32(0xFFFF0000), F32).astype(BF16)
    lo = lax.bitcast_convert_type(w << 16, F32).astype(BF16)
    return hi, lo


def _route_kernel(h_ref, g_ref, wh_ref, wl_ref, b_ref, xp_ref, r_ref):
    x = h_ref[...]
    ms = jnp.mean(x * x, axis=-1, keepdims=True)
    xn = x * lax.rsqrt(ms + NORM_EPS) * g_ref[...]
    xp_ref[...] = _pack_bf16_pairs(xn)
    xh = xn.astype(BF16)
    xl = (xn - xh.astype(F32)).astype(BF16)
    wh = wh_ref[...]
    logits = jnp.dot(xh, wh, preferred_element_type=F32) + (
        jnp.dot(xl, wh, preferred_element_type=F32) + jnp.dot(xh, wl_ref[...], preferred_element_type=F32))
    logits = logits + b_ref[...]
    lane = lax.broadcasted_iota(jnp.int32, logits.shape, 1).astype(F32)
    rmax = lambda a: jnp.max(a, axis=-1, keepdims=True)
    rmin = lambda a: jnp.min(a, axis=-1, keepdims=True)
    rsum = lambda a: jnp.sum(a, axis=-1, keepdims=True)

    gmask = lane < N_GROUPS
    gl = jnp.where(gmask, logits, NEG_INF)
    ge = jnp.where(gmask, jnp.exp(gl - rmax(gl)), 0.0)
    gp = jnp.where(gmask, ge / rsum(ge), -1.0)
    g_p = rmax(gp)
    g_idx = rmin(jnp.where(gp == g_p, lane, float(LANES)))

    lo = N_GROUPS + EXPERTS_PER_GROUP * g_idx
    emask = (lane >= lo) & (lane < lo + EXPERTS_PER_GROUP)
    el = jnp.where(emask, logits, NEG_INF)
    ee = jnp.where(emask, jnp.exp(el - rmax(el)), 0.0)
    ep = jnp.where(emask, ee / rsum(ee), -1.0)
    p1 = rmax(ep)
    i1 = rmin(jnp.where(ep == p1, lane, float(LANES)))
    ep2 = jnp.where(lane == i1, -1.0, ep)
    p2 = rmax(ep2)
    i2 = rmin(jnp.where(ep2 == p2, lane, float(LANES)))
    den = p1 + p2
    out = jnp.where(lane == 0, g_p * (p1 / den), 0.0)
    out = jnp.where(lane == 1, g_p * (p2 / den), out)
    out = jnp.where(lane == 2, i1 - N_GROUPS, out)
    out = jnp.where(lane == 3, i2 - N_GROUPS, out)
    r_ref[...] = out


def route_weights(w_group, b_group, w_expert, b_expert):
    n_pad = LANES - N_GROUPS - N_EXPERTS
    w = jnp.pad(jnp.concatenate([w_group, w_expert], axis=-1), ((0, 0), (0, 0), (0, n_pad)))
    b = jnp.pad(jnp.concatenate([b_group, b_expert], axis=-1), ((0, 0), (0, n_pad)))[:, None, :]
    wh = w.astype(BF16)
    wl = (w - wh.astype(F32)).astype(BF16)
    return wh, wl, b


def moe_route(h, g, wh, wl, b, tm=256):
    t, d = h.shape
    return pl.pallas_call(
        _route_kernel,
        grid=(t // tm,),
        in_specs=[
            pl.BlockSpec((tm, d), lambda i: (i, 0)),
            pl.BlockSpec((1, d), lambda i: (0, 0)),
            pl.BlockSpec((d, LANES), lambda i: (0, 0)),
            pl.BlockSpec((d, LANES), lambda i: (0, 0)),
            pl.BlockSpec((1, LANES), lambda i: (0, 0)),
        ],
        out_specs=[pl.BlockSpec((tm, d // 2), lambda i: (i, 0)), pl.BlockSpec((tm, LANES), lambda i: (i, 0))],
        out_shape=[jax.ShapeDtypeStruct((t, d // 2), jnp.uint32), jax.ShapeDtypeStruct((t, LANES), F32)],
        compiler_params=_params("parallel"),
        name="moe_route",
    )(h, g.reshape(1, d), wh, wl, b)


def moe_plan(expert_ids):
    t = expert_ids.shape[0]
    a = t * EXPERT_TOP_K
    n_blocks = a // MOE_BLOCK + N_EXPERTS
    cap = n_blocks * MOE_BLOCK
    flat_e = expert_ids.reshape(a)
    order = jnp.argsort(flat_e, stable=True).astype(jnp.int32)
    rank = jnp.argsort(order).astype(jnp.int32)
    counts = jnp.sum((flat_e[:, None] == jnp.arange(N_EXPERTS, dtype=jnp.int32)[None, :]).astype(jnp.int32), axis=0)
    padded = (counts + MOE_BLOCK - 1) // MOE_BLOCK * MOE_BLOCK
    start = jnp.cumsum(counts) - counts
    pend = jnp.cumsum(padded)
    pstart = pend - padded
    pos = (pstart[flat_e] + rank - start[flat_e]).astype(jnp.int32)
    block_first = jnp.arange(n_blocks, dtype=jnp.int32) * MOE_BLOCK
    block_e = jnp.minimum(jnp.sum((pend[None, :] <= block_first[:, None]).astype(jnp.int32), axis=1), N_EXPERTS - 1)
    slot = jnp.arange(cap, dtype=jnp.int32)
    slot_e = jnp.repeat(block_e, MOE_BLOCK)
    slot_rank = slot - pstart[slot_e]
    sorted_pos = jnp.clip(start[slot_e] + slot_rank, 0, a - 1)
    slot_tok = jnp.where(slot_rank < counts[slot_e], order[sorted_pos] // EXPERT_TOP_K, 0).astype(jnp.int32)
    return slot_tok, pos, block_e.astype(jnp.int32)


def _row_copy(src_ref, dst_ref, sem, src_row, dst_row):
    return pltpu.make_async_copy(src_ref.at[pl.ds(src_row, 1), :], dst_ref.at[pl.ds(dst_row, 1), :], sem)


GATHER_UNROLL = 8


def _start_row_gather(src_hbm, idx_ref, idx0, idx_stride, dst, sem, inline=False):
    n = dst.shape[0]
    if inline:
        for r in range(n):
            _row_copy(src_hbm, dst, sem, idx_ref[idx0 + r * idx_stride], r).start()
        return

    def issue(it, carry):
        for u in range(GATHER_UNROLL):
            r = it * GATHER_UNROLL + u
            _row_copy(src_hbm, dst, sem, idx_ref[idx0 + r * idx_stride], r).start()
        return carry

    lax.fori_loop(0, n // GATHER_UNROLL, issue, 0)


def _wait_row_gather(src_hbm, dst, sem):
    pltpu.make_async_copy(src_hbm.at[pl.ds(0, dst.shape[0]), :], dst, sem).wait()


def _prefetched_gather(step, start, wait):
    last = pl.num_programs(0) - 1

    @pl.when(step == 0)
    def _():
        start(0, 0, False)

    slot = step % 2
    wait(slot)
    start(jnp.minimum(step + 1, last), 1 - slot, True)
    return slot


def _drain_gather(step, wait):
    @pl.when(step == pl.num_programs(0) - 1)
    def _():
        wait(1 - step % 2)


def _moe_up_kernel(tok_ref, be_ref, x_hbm, wg_ref, wu_ref, o_ref, buf, sems):
    rb = pl.program_id(0)

    def start(block, slot, inline):
        _start_row_gather(x_hbm, tok_ref, block * MOE_BLOCK, 1, buf.at[slot], sems.at[slot], inline)

    def wait(slot):
        _wait_row_gather(x_hbm, buf.at[slot], sems.at[slot])

    slot = _prefetched_gather(rb, start, wait)
    x_hi, x_lo = _unpack_bf16_pairs(buf[slot])
    half = x_hi.shape[1]

    def proj(w_ref):
        return (jnp.dot(x_hi, w_ref[:half, :], preferred_element_type=F32)
                + jnp.dot(x_lo, w_ref[half:, :], preferred_element_type=F32))

    g = proj(wg_ref)
    u = proj(wu_ref)
    o_ref[...] = (g * (1.0 / (1.0 + jnp.exp(-g))) * u).astype(o_ref.dtype)
    _drain_gather(rb, wait)


def moe_up(xp, slot_tok, w_gate, w_up, block_e):
    d = w_gate.shape[1]
    n = w_gate.shape[2]
    cap = slot_tok.shape[0]
    w_spec = pl.BlockSpec((None, d, n), lambda rb, tok, be: (be[rb], 0, 0))
    return pl.pallas_call(
        _moe_up_kernel,
        grid_spec=pltpu.PrefetchScalarGridSpec(
            num_scalar_prefetch=2,
            grid=(cap // MOE_BLOCK,),
            in_specs=[pl.BlockSpec(memory_space=pl.ANY), w_spec, w_spec],
            out_specs=pl.BlockSpec((MOE_BLOCK, n), lambda rb, tok, be: (rb, 0)),
            scratch_shapes=[pltpu.VMEM((2, MOE_BLOCK, d // 2), jnp.uint32), pltpu.SemaphoreType.DMA((2,))],
        ),
        out_shape=jax.ShapeDtypeStruct((cap, n), BF16),
        compiler_params=_params("arbitrary"),
        name="moe_up",
    )(slot_tok, block_e, xp, w_gate, w_up)


def _moe_down_kernel(be_ref, x_ref, w_ref, o_ref, wb_ref):
    rb = pl.program_id(0)

    @pl.when((rb == 0) | (be_ref[rb] != be_ref[jnp.maximum(rb - 1, 0)]))
    def _():
        wb_ref[...] = w_ref[...].astype(BF16)

    o_ref[...] = jnp.dot(x_ref[...], wb_ref[...], preferred_element_type=F32)


def moe_down(hs, w_down, block_e):
    cap, k = hs.shape
    n = w_down.shape[2]
    return pl.pallas_call(
        _moe_down_kernel,
        grid_spec=pltpu.PrefetchScalarGridSpec(
            num_scalar_prefetch=1,
            grid=(cap // MOE_BLOCK,),
            in_specs=[
                pl.BlockSpec((MOE_BLOCK, k), lambda rb, be: (rb, 0)),
                pl.BlockSpec((None, k, n), lambda rb, be: (be[rb], 0, 0)),
            ],
            out_specs=pl.BlockSpec((MOE_BLOCK, n), lambda rb, be: (rb, 0)),
            scratch_shapes=[pltpu.VMEM((k, n), BF16)],
        ),
        out_shape=jax.ShapeDtypeStruct((cap, n), F32),
        compiler_params=_params("arbitrary"),
        name="moe_down",
    )(block_e, hs, w_down)


def _combine_kernel(pos_ref, h_ref, g_ref, gain_ref, y_hbm, o_ref, xn_ref, buf, sems, *, tm):
    i = pl.program_id(0)

    def start(block, slot, inline):
        for k in range(EXPERT_TOP_K):
            _start_row_gather(y_hbm, pos_ref, block * tm * EXPERT_TOP_K + k, EXPERT_TOP_K,
                              buf.at[slot, k], sems.at[slot, k], inline)

    def wait(slot):
        for k in range(EXPERT_TOP_K):
            _wait_row_gather(y_hbm, buf.at[slot, k], sems.at[slot, k])

    slot = _prefetched_gather(i, start, wait)
    g = g_ref[...]
    hn = h_ref[...] + (g[:, 0:1] * buf[slot, 0] + g[:, 1:2] * buf[slot, 1])
    o_ref[...] = hn
    ms = jnp.mean(hn * hn, axis=-1, keepdims=True)
    xn_ref[...] = (hn * lax.rsqrt(ms + NORM_EPS) * gain_ref[...]).astype(xn_ref.dtype)
    _drain_gather(i, wait)


def moe_combine(h, route, ys, pos, next_gain, tm=256):
    t, d = h.shape
    row = lambda i, pos: (i, 0)
    return pl.pallas_call(
        functools.partial(_combine_kernel, tm=tm),
        grid_spec=pltpu.PrefetchScalarGridSpec(
            num_scalar_prefetch=1,
            grid=(t // tm,),
            in_specs=[
                pl.BlockSpec((tm, d), row),
                pl.BlockSpec((tm, LANES), row),
                pl.BlockSpec((1, d), lambda i, pos: (0, 0)),
                pl.BlockSpec(memory_space=pl.ANY),
            ],
            out_specs=[pl.BlockSpec((tm, d), row), pl.BlockSpec((tm, d), row)],
            scratch_shapes=[pltpu.VMEM((2, EXPERT_TOP_K, tm, d), F32), pltpu.SemaphoreType.DMA((2, EXPERT_TOP_K))],
        ),
        out_shape=[jax.ShapeDtypeStruct((t, d), F32), jax.ShapeDtypeStruct((t, d), BF16)],
        compiler_params=_params("arbitrary"),
        name="moe_combine",
    )(pos, h, route, next_gain.reshape(1, d), ys)


def hierarchical_moe_update(h, g, route_wh, route_wl, route_b, w_gate, w_up, w_down, next_gain):
    xp, route = moe_route(h, g, route_wh, route_wl, route_b)
    expert_ids = route[:, 2:4].astype(jnp.int32)
    slot_tok, pos, block_e = moe_plan(expert_ids)
    hs = moe_up(xp, slot_tok, w_gate, w_up, block_e)
    ys = moe_down(hs, w_down, block_e)
    return moe_combine(h, route, ys, pos, next_gain)


def kernel(x_prompt, x_sample, p_prompt, p_sample, norm_mix, norm_ffn, a_w_qkv, a_rpb, a_w_o, b_w_qkv, b_w_o,
           c_w_qkv, c_sink, c_w_o, moe_w_group, moe_b_group, moe_w_expert, moe_b_expert, moe_w_gate, moe_w_up,
           moe_w_down, ple_w_proj, ple_norm, ple_w_gate, final_norm):
    b1, s1, d = x_prompt.shape
    b2, s2, _ = x_sample.shape
    t1 = b1 * s1
    t2 = b2 * s2
    assert t1 % s2 == 0
    seqs = ((b1, 0, s1), (b2, t1 // s2, s2))

    h = jnp.concatenate([x_prompt.reshape(t1, d), x_sample.reshape(t2, d)], axis=0)
    p = jnp.concatenate([p_prompt.reshape(DEPTH, t1, D_PLE), p_sample.reshape(DEPTH, t2, D_PLE)], axis=1).astype(BF16)
    slopes = 2.0 ** (-8.0 * jnp.arange(1, N_HEADS + 1, dtype=F32) / N_HEADS)
    route_wh, route_wl, route_b = route_weights(moe_w_group, moe_b_group, moe_w_expert, moe_b_expert)

    for i in range(DEPTH):
        kind = i % N_MIXERS
        j = i // N_MIXERS
        xn = rmsnorm(h, norm_mix[i], BF16)
        if kind == 0:
            qkv = matmul(xn, a_w_qkv[j].astype(BF16), BF16)
            slabs = na_bias_slabs(a_rpb[j])
            o = [neighbourhood_attention(qkv, slabs, n, off, s) for n, off, s in seqs]
            w_o = a_w_o[j]
        elif kind == 1:
            qkv = matmul(xn, b_w_qkv[j].astype(BF16), F32)
            o = [dilated_attention(qkv, slopes, n, off, s) for n, off, s in seqs]
            w_o = b_w_o[j]
        else:
            qkv = matmul(xn, c_w_qkv[j].astype(BF16), BF16)
            o = [gqa_sink_attention(qkv, slopes, c_sink[j], n, off, s) for n, off, s in seqs]
            w_o = c_w_o[j]
        h = matmul_residual(o[0], o[1], w_o.astype(BF16), h)
        h, xn = hierarchical_moe_update(
            h, norm_ffn[i], route_wh[i], route_wl[i], route_b[i],
            moe_w_gate[i].astype(BF16), moe_w_up[i].astype(BF16), moe_w_down[i], ple_norm[i])
        h = ple_update(xn, ple_w_gate[i].astype(BF16), p[i], ple_w_proj[i].astype(BF16), h)

    y1 = rmsnorm(h, final_norm, F32, 0, t1)
    y2 = rmsnorm(h, final_norm, F32, t1, t2)
    return y1.reshape(b1, s1, d), y2.reshape(b2, s2, d)
```

```python
import functools

import jax
import jax.numpy as jnp
from jax import lax
from jax.experimental import pallas as pl
from jax.experimental.pallas import tpu as pltpu

D_MODEL = 4096
HEAD_DIM = 128
N_HEADS = D_MODEL // HEAD_DIM
DEPTH = 4
N_MIXERS = 3
GRID_W = 64
NA_ROWS = 8
NA_COLS = 16
DIL_PATTERNS = ((128, 1), (512, 4), (2048, 16))
C_KV_HEADS = N_HEADS // 4
C_GROUP = N_HEADS // C_KV_HEADS
C_WINDOW = 128
N_GROUPS = 4
EXPERTS_PER_GROUP = 8
N_EXPERTS = N_GROUPS * EXPERTS_PER_GROUP
EXPERT_TOP_K = 2
D_EXPERT = D_MODEL // 4
D_PLE = 256
NORM_EPS = 1e-6
NEG_INF = -1e30
SCALE = HEAD_DIM ** -0.5

LANES = 128
VMEM_LIMIT = 56 * 1024 * 1024
MOE_BLOCK = 256
F32 = jnp.float32
BF16 = jnp.bfloat16


def _params(*sem):
    return pltpu.CompilerParams(dimension_semantics=sem, vmem_limit_bytes=VMEM_LIMIT)


def _rmsnorm_kernel(x_ref, g_ref, o_ref):
    x = x_ref[...]
    ms = jnp.mean(x * x, axis=-1, keepdims=True)
    o_ref[...] = (x * lax.rsqrt(ms + NORM_EPS) * g_ref[...]).astype(o_ref.dtype)


def rmsnorm(x, g, out_dtype, row0=0, m=None, tm=256):
    d = x.shape[1]
    m = x.shape[0] if m is None else m
    b0 = row0 // tm
    return pl.pallas_call(
        _rmsnorm_kernel,
        grid=(m // tm,),
        in_specs=[pl.BlockSpec((tm, d), lambda i: (b0 + i, 0)), pl.BlockSpec((1, d), lambda i: (0, 0))],
        out_specs=pl.BlockSpec((tm, d), lambda i: (i, 0)),
        out_shape=jax.ShapeDtypeStruct((m, d), out_dtype),
        compiler_params=_params("parallel"),
        name="rmsnorm",
    )(x, g.reshape(1, d))


def _mm_kernel(x_ref, w_ref, o_ref):
    o_ref[...] = jnp.dot(x_ref[...], w_ref[...], preferred_element_type=F32).astype(o_ref.dtype)


def _layer_spec(block, index_map, layer):
    return pl.BlockSpec((None,) + tuple(block), lambda *a: (layer,) + tuple(index_map(*a)))


def matmul(x, w, layer, out_dtype, tm=1024, tn=1024):
    m, k = x.shape
    n = w.shape[2]
    return pl.pallas_call(
        _mm_kernel,
        grid=(m // tm, n // tn),
        in_specs=[pl.BlockSpec((tm, k), lambda i, j: (i, 0)), _layer_spec((k, tn), lambda i, j: (0, j), layer)],
        out_specs=pl.BlockSpec((tm, tn), lambda i, j: (i, j)),
        out_shape=jax.ShapeDtypeStruct((m, n), out_dtype),
        compiler_params=_params("parallel", "parallel"),
        name="matmul",
    )(x, w)


def _mm_res_kernel(xa_ref, xb_ref, w_ref, r_ref, o_ref, *, na):
    @pl.when(pl.program_id(0) < na)
    def _():
        o_ref[...] = r_ref[...] + jnp.dot(xa_ref[...], w_ref[...], preferred_element_type=F32)

    @pl.when(pl.program_id(0) >= na)
    def _():
        o_ref[...] = r_ref[...] + jnp.dot(xb_ref[...], w_ref[...], preferred_element_type=F32)


def matmul_residual(xa, xb, w, layer, res, tm=512, tn=1024):
    ma, k = xa.shape
    mb = xb.shape[0]
    n = w.shape[2]
    na = ma // tm
    return pl.pallas_call(
        functools.partial(_mm_res_kernel, na=na),
        grid=((ma + mb) // tm, n // tn),
        in_specs=[
            pl.BlockSpec((tm, k), lambda i, j: (jnp.minimum(i, na - 1), 0)),
            pl.BlockSpec((tm, k), lambda i, j: (jnp.maximum(i - na, 0), 0)),
            _layer_spec((k, tn), lambda i, j: (0, j), layer),
            pl.BlockSpec((tm, tn), lambda i, j: (i, j)),
        ],
        out_specs=pl.BlockSpec((tm, tn), lambda i, j: (i, j)),
        out_shape=jax.ShapeDtypeStruct((ma + mb, n), F32),
        compiler_params=_params("arbitrary", "arbitrary"),
        name="matmul_residual",
    )(xa, xb, w, res)


def _ple_kernel(x_ref, wg_ref, p_ref, wp_ref, r_ref, o_ref):
    gate_logit = jnp.dot(x_ref[...], wg_ref[...], preferred_element_type=F32)
    proj = jnp.dot(p_ref[...], wp_ref[...], preferred_element_type=F32)
    gate = 1.0 / (1.0 + jnp.exp(-gate_logit))
    o_ref[...] = r_ref[...] + gate * proj


def ple_update(xn, w_gate, p, w_proj, layer, res, tm=1024, tn=512):
    m, k = xn.shape
    n = w_gate.shape[2]
    kp = p.shape[2]
    return pl.pallas_call(
        _ple_kernel,
        grid=(m // tm, n // tn),
        in_specs=[
            pl.BlockSpec((tm, k), lambda i, j: (i, 0)),
            _layer_spec((k, tn), lambda i, j: (0, j), layer),
            _layer_spec((tm, kp), lambda i, j: (i, 0), layer),
            _layer_spec((kp, tn), lambda i, j: (0, j), layer),
            pl.BlockSpec((tm, tn), lambda i, j: (i, j)),
        ],
        out_specs=pl.BlockSpec((tm, tn), lambda i, j: (i, j)),
        out_shape=jax.ShapeDtypeStruct((m, n), F32),
        compiler_params=_params("parallel", "parallel"),
        name="ple_update",
    )(xn, w_gate, p, w_proj, res)


def _qk_scores(q, k):
    return lax.dot_general(q, k, (((1,), (1,)), ((), ())), preferred_element_type=F32)


NA_QROWS = 4
NA_KROWS = NA_QROWS + NA_ROWS
NA_VARIANTS = 3


def _na_kernel(q_ref, k_ref, v_ref, b_ref, o_ref, *, rows, unroll):
    nq = NA_QROWS * GRID_W
    nk = NA_KROWS * GRID_W
    nblk = rows // NA_QROWS

    def body(it, carry):
        for u in range(unroll):
            blk = it * unroll + u
            r0 = blk * NA_QROWS
            ks = jnp.clip(r0 - NA_ROWS // 2, 0, rows - NA_KROWS)
            variant = jnp.where(blk == 0, 0, jnp.where(blk == nblk - 1, 2, 1))
            q0 = pl.multiple_of(r0 * GRID_W, nq)
            k0 = pl.multiple_of(ks * GRID_W, GRID_W)
            q = q_ref[pl.ds(q0, nq), :]
            k = k_ref[pl.ds(k0, nk), :]
            v = v_ref[pl.ds(k0, nk), :]
            s = _qk_scores(q, k) * SCALE + b_ref[variant]
            m = jnp.max(s, axis=-1, keepdims=True)
            e = jnp.exp(s - m)
            l = jnp.sum(e, axis=-1, keepdims=True)
            o = jnp.dot(e.astype(BF16), v, preferred_element_type=F32) / l
            o_ref[pl.ds(q0, nq), :] = o.astype(o_ref.dtype)
        return carry

    lax.fori_loop(0, nblk // unroll, body, 0)


def _na_slab_tables():
    import numpy as np
    rowsel = np.zeros((NA_VARIANTS, NA_QROWS, NA_KROWS, 2 * NA_ROWS - 1), np.float32)
    half = NA_ROWS // 2
    for v in range(NA_VARIANTS):
        for i in range(NA_QROWS):
            k_first = (0, -half, NA_QROWS - NA_KROWS)[v]
            w_first = (0, i - half, NA_QROWS - NA_ROWS)[v]
            for a in range(NA_KROWS):
                if 0 <= k_first + a - w_first < NA_ROWS:
                    rowsel[v, i, a, k_first + a - i + NA_ROWS - 1] = 1.0
    colsel = np.zeros((GRID_W, GRID_W, 2 * NA_COLS - 1), np.float32)
    for qc in range(GRID_W):
        cstart = min(max(qc - NA_COLS // 2, 0), GRID_W - NA_COLS)
        for kc in range(cstart, cstart + NA_COLS):
            colsel[qc, kc, kc - qc + NA_COLS - 1] = 1.0
    return rowsel, colsel


def na_bias_slabs(rpb):
    rowsel, colsel = _na_slab_tables()
    hi = lax.Precision.HIGHEST
    t = jnp.einsum("hrc,viar->hviac", rpb, rowsel, precision=hi)
    t = jnp.einsum("hviac,qkc->hviqak", t, colsel, precision=hi)
    ok = (rowsel.sum(-1)[:, :, None, :, None] * colsel.sum(-1)[None, None, :, None, :]) > 0
    t = jnp.where(ok[None], t, NEG_INF)
    return t.reshape(rpb.shape[0], NA_VARIANTS, NA_QROWS * GRID_W, NA_KROWS * GRID_W)


def neighbourhood_attention(qkv, slabs, n_seq, seq_off, s):
    rows = s // GRID_W
    assert rows % (2 * NA_QROWS) == 0 and rows >= NA_KROWS + NA_QROWS
    blk = lambda c0: pl.BlockSpec((s, HEAD_DIM), lambda b, h: (seq_off + b, c0 + h))
    return pl.pallas_call(
        functools.partial(_na_kernel, rows=rows, unroll=2),
        grid=(n_seq, N_HEADS),
        in_specs=[
            blk(0), blk(N_HEADS), blk(2 * N_HEADS),
            pl.BlockSpec((None, NA_VARIANTS, NA_QROWS * GRID_W, NA_KROWS * GRID_W), lambda b, h: (h, 0, 0, 0)),
        ],
        out_specs=pl.BlockSpec((s, HEAD_DIM), lambda b, h: (b, h)),
        out_shape=jax.ShapeDtypeStruct((n_seq * s, D_MODEL), BF16),
        compiler_params=_params("parallel", "parallel"),
        name="neighbourhood_attention",
    )(qkv, qkv, qkv, slabs)


def _dil_kernel(slope_ref, q_ref, k_ref, v_ref, o_ref, acc_ref, m_ref, l_ref, *, s):
    slope = slope_ref[pl.program_id(1)]
    for bi, (window, dil) in enumerate(sorted(DIL_PATTERNS, key=lambda p: -p[1])):
        half = window // (2 * dil)
        sub = s // dil
        qb = min(sub, 256)
        kb = sub if sub <= 256 else qb + 2 * half
        nch = sub // qb
        stride = dil if dil > 1 else None

        def chunk(it, dil=dil, half=half, sub=sub, qb=qb, kb=kb, nch=nch, stride=stride):
            res = it // nch
            u0 = (it % nch) * qb
            ks = jnp.clip(u0 - half, 0, sub - kb)
            q_idx = pl.ds(res + u0 * dil, qb, stride=stride)
            k_idx = pl.ds(res + ks * dil, kb, stride=stride)
            q = (q_ref[q_idx, :] * SCALE).astype(BF16)
            k = k_ref[k_idx, :].astype(BF16)
            v = v_ref[k_idx, :].astype(BF16)
            uq = u0 + lax.broadcasted_iota(jnp.int32, (qb, kb), 0)
            uk = ks + lax.broadcasted_iota(jnp.int32, (qb, kb), 1)
            rel = jnp.abs(uk - uq)
            sc = _qk_scores(q, k) - (slope * dil) * rel.astype(F32)
            sc = jnp.where(rel <= half, sc, NEG_INF)
            m_i = jnp.max(sc, axis=-1, keepdims=True)
            e = jnp.exp(sc - m_i)
            l_i = jnp.sum(e, axis=-1, keepdims=True)
            a_i = jnp.dot(e.astype(BF16), v, preferred_element_type=F32)
            return q_idx, a_i, jnp.broadcast_to(m_i, (qb, HEAD_DIM)), jnp.broadcast_to(l_i, (qb, HEAD_DIM))

        unroll = 4 if qb * kb <= 128 * 128 else 2

        def body(it, carry, bi=bi, chunk=chunk, unroll=unroll):
            for q_idx, a_i, m_b, l_b in [chunk(it * unroll + u) for u in range(unroll)]:
                if bi == 0:
                    acc_ref[q_idx, :] = a_i
                    m_ref[q_idx, :] = m_b
                    l_ref[q_idx, :] = l_b
                else:
                    m_o = m_ref[q_idx, :]
                    m_n = jnp.maximum(m_o, m_b)
                    w_o = jnp.exp(m_o - m_n)
                    w_i = jnp.exp(m_b - m_n)
                    acc_ref[q_idx, :] = w_o * acc_ref[q_idx, :] + w_i * a_i
                    l_ref[q_idx, :] = w_o * l_ref[q_idx, :] + w_i * l_b
                    m_ref[q_idx, :] = m_n
            return carry

        assert (dil * nch) % unroll == 0
        lax.fori_loop(0, dil * nch // unroll, body, 0)
    o_ref[...] = (acc_ref[...] / l_ref[...]).astype(o_ref.dtype)


def dilated_attention(qkv, slopes, n_seq, seq_off, s):
    blk = lambda c0: pl.BlockSpec((s, HEAD_DIM), lambda b, h: (seq_off + b, c0 + h))
    return pl.pallas_call(
        functools.partial(_dil_kernel, s=s),
        grid=(n_seq, N_HEADS),
        in_specs=[pl.BlockSpec(memory_space=pltpu.SMEM), blk(0), blk(N_HEADS), blk(2 * N_HEADS)],
        out_specs=pl.BlockSpec((s, HEAD_DIM), lambda b, h: (b, h)),
        out_shape=jax.ShapeDtypeStruct((n_seq * s, D_MODEL), BF16),
        scratch_shapes=[pltpu.VMEM((s, HEAD_DIM), F32)] * 3,
        compiler_params=_params("parallel", "parallel"),
        name="dilated_attention",
    )(slopes, qkv, qkv, qkv)


def _gqa_kernel(slope_ref, sink_ref, q_ref, k_ref, v_ref, o_ref, *, s, qb):
    kvh = pl.program_id(1)
    kb = qb + 2 * C_WINDOW

    def body(n, carry):
        q0 = pl.multiple_of(n * qb, qb)
        ks = pl.multiple_of(jnp.clip(n * qb - C_WINDOW, 0, s - kb), C_WINDOW)
        k = k_ref[pl.ds(ks, kb), :]
        v = v_ref[pl.ds(ks, kb), :]
        tq = q0 + lax.broadcasted_iota(jnp.int32, (qb, kb), 0)
        tk = ks + lax.broadcasted_iota(jnp.int32, (qb, kb), 1)
        rel = jnp.abs(tk - tq)
        valid = rel <= C_WINDOW
        relf = rel.astype(F32)
        for g in range(C_GROUP):
            head = kvh * C_GROUP + g
            slope = slope_ref[head]
            sink = sink_ref[head]
            q = q_ref[pl.ds(q0, qb), g * HEAD_DIM:(g + 1) * HEAD_DIM]
            sc = _qk_scores(q, k) * SCALE - slope * relf
            sc = jnp.where(valid, sc, NEG_INF)
            m = jnp.maximum(jnp.max(sc, axis=-1, keepdims=True), sink)
            e = jnp.exp(sc - m)
            l = jnp.sum(e, axis=-1, keepdims=True) + jnp.exp(sink - m)
            o = jnp.dot(e.astype(BF16), v, preferred_element_type=F32) / l
            o_ref[pl.ds(q0, qb), g * HEAD_DIM:(g + 1) * HEAD_DIM] = o.astype(o_ref.dtype)
        return carry

    lax.fori_loop(0, s // qb, body, 0)


def gqa_sink_attention(qkv, slopes, sink, n_seq, seq_off, s):
    gw = C_GROUP * HEAD_DIM
    kv0 = D_MODEL // HEAD_DIM
    smem = pl.BlockSpec(memory_space=pltpu.SMEM)
    return pl.pallas_call(
        functools.partial(_gqa_kernel, s=s, qb=256),
        grid=(n_seq, C_KV_HEADS),
        in_specs=[
            smem, smem,
            pl.BlockSpec((s, gw), lambda b, h: (seq_off + b, h)),
            pl.BlockSpec((s, HEAD_DIM), lambda b, h: (seq_off + b, kv0 + h)),
            pl.BlockSpec((s, HEAD_DIM), lambda b, h: (seq_off + b, kv0 + C_KV_HEADS + h)),
        ],
        out_specs=pl.BlockSpec((s, gw), lambda b, h: (b, h)),
        out_shape=jax.ShapeDtypeStruct((n_seq * s, D_MODEL), BF16),
        compiler_params=_params("parallel", "parallel"),
        name="gqa_sink_attention",
    )(slopes, sink, qkv, qkv, qkv)


def _pack_bf16_pairs(x):
    half = x.shape[1] // 2
    hi = lax.bitcast_convert_type(x[:, :half].astype(BF16).astype(F32), jnp.uint32)
    lo = lax.bitcast_convert_type(x[:, half:].astype(BF16).astype(F32), jnp.uint32)
    return hi | (lo >> 16)


def _unpack_bf16_pairs(w):
    hi = lax.bitcast_convert_type(w & jnp.uint32(0xFFFF0000), F32)
    lo = lax.bitcast_convert_type(w << 16, F32)
    return hi, lo


def _route_kernel(h_ref, g_ref, wh_ref, wl_ref, b_ref, xp_ref, r_ref):
    x = h_ref[...]
    ms = jnp.mean(x * x, axis=-1, keepdims=True)
    xn = x * lax.rsqrt(ms + NORM_EPS) * g_ref[...]
    xp_ref[...] = _pack_bf16_pairs(xn)
    xh = xn.astype(BF16)
    xl = (xn - xh.astype(F32)).astype(BF16)
    wh = wh_ref[...]
    logits = jnp.dot(xh, wh, preferred_element_type=F32) + (
        jnp.dot(xl, wh, preferred_element_type=F32) + jnp.dot(xh, wl_ref[...], preferred_element_type=F32))
    logits = logits + b_ref[...]
    lane = lax.broadcasted_iota(jnp.int32, logits.shape, 1).astype(F32)
    rmax = lambda a: jnp.max(a, axis=-1, keepdims=True)
    rmin = lambda a: jnp.min(a, axis=-1, keepdims=True)
    rsum = lambda a: jnp.sum(a, axis=-1, keepdims=True)

    gmask = lane < N_GROUPS
    gl = jnp.where(gmask, logits, NEG_INF)
    ge = jnp.where(gmask, jnp.exp(gl - rmax(gl)), 0.0)
    gp = jnp.where(gmask, ge / rsum(ge), -1.0)
    g_p = rmax(gp)
    g_idx = rmin(jnp.where(gp == g_p, lane, float(LANES)))

    lo = N_GROUPS + EXPERTS_PER_GROUP * g_idx
    emask = (lane >= lo) & (lane < lo + EXPERTS_PER_GROUP)
    el = jnp.where(emask, logits, NEG_INF)
    ee = jnp.where(emask, jnp.exp(el - rmax(el)), 0.0)
    ep = jnp.where(emask, ee / rsum(ee), -1.0)
    p1 = rmax(ep)
    i1 = rmin(jnp.where(ep == p1, lane, float(LANES)))
    ep2 = jnp.where(lane == i1, -1.0, ep)
    p2 = rmax(ep2)
    i2 = rmin(jnp.where(ep2 == p2, lane, float(LANES)))
    den = p1 + p2
    out = jnp.where(lane == 0, g_p * (p1 / den), 0.0)
    out = jnp.where(lane == 1, g_p * (p2 / den), out)
    out = jnp.where(lane == 2, i1 - N_GROUPS, out)
    out = jnp.where(lane == 3, i2 - N_GROUPS, out)
    r_ref[...] = out


def route_weights(w_group, b_group, w_expert, b_expert):
    n_pad = LANES - N_GROUPS - N_EXPERTS
    w = jnp.pad(jnp.concatenate([w_group, w_expert], axis=-1), ((0, 0), (0, 0), (0, n_pad)))
    b = jnp.pad(jnp.concatenate([b_group, b_expert], axis=-1), ((0, 0), (0, n_pad)))[:, None, :]
    wh = w.astype(BF16)
    wl = (w - wh.astype(F32)).astype(BF16)
    return wh, wl, b


def moe_route(h, g, wh, wl, b, layer, tm=256):
    t, d = h.shape
    return pl.pallas_call(
        _route_kernel,
        grid=(t // tm,),
        in_specs=[
            pl.BlockSpec((tm, d), lambda i: (i, 0)),
            pl.BlockSpec((1, d), lambda i: (0, 0)),
            _layer_spec((d, LANES), lambda i: (0, 0), layer),
            _layer_spec((d, LANES), lambda i: (0, 0), layer),
            _layer_spec((1, LANES), lambda i: (0, 0), layer),
        ],
        out_specs=[pl.BlockSpec((tm, d // 2), lambda i: (i, 0)), pl.BlockSpec((tm, LANES), lambda i: (i, 0))],
        out_shape=[jax.ShapeDtypeStruct((t, d // 2), jnp.uint32), jax.ShapeDtypeStruct((t, LANES), F32)],
        compiler_params=_params("parallel"),
        name="moe_route",
    )(h, g.reshape(1, d), wh, wl, b)


def moe_plan(expert_ids):
    t = expert_ids.shape[0]
    a = t * EXPERT_TOP_K
    n_blocks = a // MOE_BLOCK + N_EXPERTS
    cap = n_blocks * MOE_BLOCK
    flat_e = expert_ids.reshape(a)
    order = jnp.argsort(flat_e, stable=True).astype(jnp.int32)
    rank = jnp.argsort(order).astype(jnp.int32)
    counts = jnp.sum((flat_e[:, None] == jnp.arange(N_EXPERTS, dtype=jnp.int32)[None, :]).astype(jnp.int32), axis=0)
    padded = (counts + MOE_BLOCK - 1) // MOE_BLOCK * MOE_BLOCK
    start = jnp.cumsum(counts) - counts
    pend = jnp.cumsum(padded)
    pstart = pend - padded
    pos = (pstart[flat_e] + rank - start[flat_e]).astype(jnp.int32)
    block_first = jnp.arange(n_blocks, dtype=jnp.int32) * MOE_BLOCK
    block_e = jnp.minimum(jnp.sum((pend[None, :] <= block_first[:, None]).astype(jnp.int32), axis=1), N_EXPERTS - 1)
    slot = jnp.arange(cap, dtype=jnp.int32)
    slot_e = jnp.repeat(block_e, MOE_BLOCK)
    slot_rank = slot - pstart[slot_e]
    sorted_pos = jnp.clip(start[slot_e] + slot_rank, 0, a - 1)
    slot_tok = jnp.where(slot_rank < counts[slot_e], order[sorted_pos] // EXPERT_TOP_K, 0).astype(jnp.int32)
    n_used = (pend[N_EXPERTS - 1:] // MOE_BLOCK).astype(jnp.int32)
    return slot_tok, pos, block_e.astype(jnp.int32), n_used


def _row_copy(src_ref, dst_ref, sem, src_row, dst_row):
    return pltpu.make_async_copy(src_ref.at[pl.ds(src_row, 1), :], dst_ref.at[pl.ds(dst_row, 1), :], sem)


GATHER_UNROLL = 8


def _start_row_gather(src_hbm, idx_ref, idx0, idx_stride, dst, sem, inline=False):
    n = dst.shape[0]
    if inline:
        for r in range(n):
            _row_copy(src_hbm, dst, sem, idx_ref[idx0 + r * idx_stride], r).start()
        return

    def issue(it, carry):
        for u in range(GATHER_UNROLL):
            r = it * GATHER_UNROLL + u
            _row_copy(src_hbm, dst, sem, idx_ref[idx0 + r * idx_stride], r).start()
        return carry

    lax.fori_loop(0, n // GATHER_UNROLL, issue, 0)


def _wait_row_gather(src_hbm, dst, sem):
    pltpu.make_async_copy(src_hbm.at[pl.ds(0, dst.shape[0]), :], dst, sem).wait()


def _prefetched_gather(step, n_steps, start, wait):
    @pl.when(step == 0)
    def _():
        start(0, 0, False)

    slot = step % 2
    wait(slot)
    start(jnp.minimum(step + 1, n_steps - 1), 1 - slot, True)
    return slot


def _drain_gather(step, n_steps, wait):
    @pl.when(step == n_steps - 1)
    def _():
        wait(1 - step % 2)


def _moe_up_kernel(tok_ref, be_ref, nu_ref, x_hbm, wg_ref, wu_ref, o_ref, buf, sems):
    rb = pl.program_id(0)
    n_used = nu_ref[0]

    def start(block, slot, inline):
        _start_row_gather(x_hbm, tok_ref, block * MOE_BLOCK, 1, buf.at[slot], sems.at[slot], inline)

    def wait(slot):
        _wait_row_gather(x_hbm, buf.at[slot], sems.at[slot])

    @pl.when(rb < n_used)
    def _():
        slot = _prefetched_gather(rb, n_used, start, wait)
        x_hi, x_lo = _unpack_bf16_pairs(buf[slot])
        x_hi = x_hi.astype(BF16)
        x_lo = x_lo.astype(BF16)
        half = x_hi.shape[1]

        def proj(w_ref):
            return (jnp.dot(x_hi, w_ref[:half, :], preferred_element_type=F32)
                    + jnp.dot(x_lo, w_ref[half:, :], preferred_element_type=F32))

        g = proj(wg_ref)
        u = proj(wu_ref)
        o_ref[...] = (g * (1.0 / (1.0 + jnp.exp(-g))) * u).astype(o_ref.dtype)
        _drain_gather(rb, n_used, wait)

    @pl.when(rb >= n_used)
    def _():
        o_ref[...] = jnp.zeros_like(o_ref)


def moe_up(xp, slot_tok, w_gate, w_up, layer, block_e, n_used):
    d = w_gate.shape[2]
    n = w_gate.shape[3]
    cap = slot_tok.shape[0]
    w_spec = pl.BlockSpec((None, None, d, n), lambda rb, tok, be, nu: (layer, be[rb], 0, 0))
    return pl.pallas_call(
        _moe_up_kernel,
        grid_spec=pltpu.PrefetchScalarGridSpec(
            num_scalar_prefetch=3,
            grid=(cap // MOE_BLOCK,),
            in_specs=[pl.BlockSpec(memory_space=pl.ANY), w_spec, w_spec],
            out_specs=pl.BlockSpec((MOE_BLOCK, n), lambda rb, tok, be, nu: (rb, 0)),
            scratch_shapes=[pltpu.VMEM((2, MOE_BLOCK, d // 2), jnp.uint32), pltpu.SemaphoreType.DMA((2,))],
        ),
        out_shape=jax.ShapeDtypeStruct((cap, n), BF16),
        compiler_params=_params("arbitrary"),
        name="moe_up",
    )(slot_tok, block_e, n_used, xp, w_gate, w_up)


def _moe_down_kernel(be_ref, nu_ref, x_ref, w_ref, o_ref, wb_ref):
    rb = pl.program_id(0)
    n_used = nu_ref[0]

    @pl.when(rb < n_used)
    def _():
        @pl.when((rb == 0) | (be_ref[rb] != be_ref[jnp.maximum(rb - 1, 0)]))
        def _():
            wb_ref[...] = w_ref[...].astype(BF16)

        o_ref[...] = _pack_bf16_pairs(jnp.dot(x_ref[...], wb_ref[...], preferred_element_type=F32))

    @pl.when(rb >= n_used)
    def _():
        o_ref[...] = jnp.zeros_like(o_ref)


def moe_down(hs, w_down, layer, block_e, n_used):
    cap, k = hs.shape
    n = w_down.shape[3]
    return pl.pallas_call(
        _moe_down_kernel,
        grid_spec=pltpu.PrefetchScalarGridSpec(
            num_scalar_prefetch=2,
            grid=(cap // MOE_BLOCK,),
            in_specs=[
                pl.BlockSpec((MOE_BLOCK, k), lambda rb, be, nu: (rb, 0)),
                pl.BlockSpec((None, None, k, n), lambda rb, be, nu: (layer, be[rb], 0, 0)),
            ],
            out_specs=pl.BlockSpec((MOE_BLOCK, n // 2), lambda rb, be, nu: (rb, 0)),
            scratch_shapes=[pltpu.VMEM((k, n), BF16)],
        ),
        out_shape=jax.ShapeDtypeStruct((cap, n // 2), jnp.uint32),
        compiler_params=_params("arbitrary"),
        name="moe_down",
    )(block_e, n_used, hs, w_down)


def _combine_kernel(pos_ref, h_ref, g_ref, gain_ref, y_hbm, o_ref, xn_ref, buf, sems, *, tm):
    i = pl.program_id(0)
    n_steps = pl.num_programs(0)

    def start(block, slot, inline):
        for k in range(EXPERT_TOP_K):
            _start_row_gather(y_hbm, pos_ref, block * tm * EXPERT_TOP_K + k, EXPERT_TOP_K,
                              buf.at[slot, k], sems.at[slot, k], inline)

    def wait(slot):
        for k in range(EXPERT_TOP_K):
            _wait_row_gather(y_hbm, buf.at[slot, k], sems.at[slot, k])

    slot = _prefetched_gather(i, n_steps, start, wait)
    g = g_ref[...]
    half = h_ref.shape[1] // 2
    y0 = _unpack_bf16_pairs(buf[slot, 0])
    y1 = _unpack_bf16_pairs(buf[slot, 1])
    hn = [h_ref[:, c * half:(c + 1) * half] + (g[:, 0:1] * y0[c] + g[:, 1:2] * y1[c]) for c in range(2)]
    ssq = sum(jnp.sum(x * x, axis=-1, keepdims=True) for x in hn)
    r = lax.rsqrt(ssq / h_ref.shape[1] + NORM_EPS)
    for c in range(2):
        cols = slice(c * half, (c + 1) * half)
        o_ref[:, cols] = hn[c]
        xn_ref[:, cols] = (hn[c] * r * gain_ref[:, cols]).astype(xn_ref.dtype)
    _drain_gather(i, n_steps, wait)


def moe_combine(h, route, ys, pos, next_gain, tm=256):
    t, d = h.shape
    row = lambda i, pos: (i, 0)
    return pl.pallas_call(
        functools.partial(_combine_kernel, tm=tm),
        grid_spec=pltpu.PrefetchScalarGridSpec(
            num_scalar_prefetch=1,
            grid=(t // tm,),
            in_specs=[
                pl.BlockSpec((tm, d), row),
                pl.BlockSpec((tm, LANES), row),
                pl.BlockSpec((1, d), lambda i, pos: (0, 0)),
                pl.BlockSpec(memory_space=pl.ANY),
            ],
            out_specs=[pl.BlockSpec((tm, d), row), pl.BlockSpec((tm, d), row)],
            scratch_shapes=[pltpu.VMEM((2, EXPERT_TOP_K, tm, d // 2), jnp.uint32),
                            pltpu.SemaphoreType.DMA((2, EXPERT_TOP_K))],
        ),
        out_shape=[jax.ShapeDtypeStruct((t, d), F32), jax.ShapeDtypeStruct((t, d), BF16)],
        compiler_params=_params("arbitrary"),
        name="moe_combine",
    )(pos, h, route, next_gain.reshape(1, d), ys)


def hierarchical_moe_update(h, g, route_wh, route_wl, route_b, w_gate, w_up, w_down, layer, next_gain):
    xp, route = moe_route(h, g, route_wh, route_wl, route_b, layer)
    expert_ids = route[:, 2:4].astype(jnp.int32)
    slot_tok, pos, block_e, n_used = moe_plan(expert_ids)
    hs = moe_up(xp, slot_tok, w_gate, w_up, layer, block_e, n_used)
    ys = moe_down(hs, w_down, layer, block_e, n_used)
    return moe_combine(h, route, ys, pos, next_gain)


def kernel(x_prompt, x_sample, p_prompt, p_sample, norm_mix, norm_ffn, a_w_qkv, a_rpb, a_w_o, b_w_qkv, b_w_o,
           c_w_qkv, c_sink, c_w_o, moe_w_group, moe_b_group, moe_w_expert, moe_b_expert, moe_w_gate, moe_w_up,
           moe_w_down, ple_w_proj, ple_norm, ple_w_gate, final_norm):
    b1, s1, d = x_prompt.shape
    b2, s2, _ = x_sample.shape
    t1 = b1 * s1
    t2 = b2 * s2
    assert t1 % s2 == 0
    seqs = ((b1, 0, s1), (b2, t1 // s2, s2))

    h = jnp.concatenate([x_prompt.reshape(t1, d), x_sample.reshape(t2, d)], axis=0)
    p = jnp.concatenate([p_prompt.reshape(DEPTH, t1, D_PLE), p_sample.reshape(DEPTH, t2, D_PLE)], axis=1).astype(BF16)
    slopes = 2.0 ** (-8.0 * jnp.arange(1, N_HEADS + 1, dtype=F32) / N_HEADS)
    route_wh, route_wl, route_b = route_weights(moe_w_group, moe_b_group, moe_w_expert, moe_b_expert)

    w_qkv = [w.astype(BF16) for w in (a_w_qkv, b_w_qkv, c_w_qkv)]
    w_out = [w.astype(BF16) for w in (a_w_o, b_w_o, c_w_o)]
    w_gate = moe_w_gate.astype(BF16)
    w_up = moe_w_up.astype(BF16)
    w_ple_gate = ple_w_gate.astype(BF16)
    w_ple_proj = ple_w_proj.astype(BF16)

    for i in range(DEPTH):
        kind = i % N_MIXERS
        j = i // N_MIXERS
        xn = rmsnorm(h, norm_mix[i], BF16)
        if kind == 0:
            qkv = matmul(xn, w_qkv[kind], j, BF16)
            slabs = na_bias_slabs(a_rpb[j])
            o = [neighbourhood_attention(qkv, slabs, n, off, s) for n, off, s in seqs]
        elif kind == 1:
            qkv = matmul(xn, w_qkv[kind], j, F32)
            o = [dilated_attention(qkv, slopes, n, off, s) for n, off, s in seqs]
        else:
            qkv = matmul(xn, w_qkv[kind], j, BF16)
            o = [gqa_sink_attention(qkv, slopes, c_sink[j], n, off, s) for n, off, s in seqs]
        h = matmul_residual(o[0], o[1], w_out[kind], j, h)
        h, xn = hierarchical_moe_update(
            h, norm_ffn[i], route_wh, route_wl, route_b, w_gate, w_up, moe_w_down, i, ple_norm[i])
        h = ple_update(xn, w_ple_gate, p, w_ple_proj, i, h)

    y1 = rmsnorm(h, final_norm, F32, 0, t1)
    y2 = rmsnorm(h, final_norm, F32, t1, t2)
    return y1.reshape(b1, s1, d), y2.reshape(b2, s2, d)
```

```python
import functools

import jax
import jax.numpy as jnp
from jax import lax
from jax.experimental import pallas as pl
from jax.experimental.pallas import tpu as pltpu

D_MODEL = 4096
HEAD_DIM = 128
N_HEADS = D_MODEL // HEAD_DIM
DEPTH = 4
N_MIXERS = 3
GRID_W = 64
NA_ROWS = 8
NA_COLS = 16
DIL_PATTERNS = ((128, 1), (512, 4), (2048, 16))
C_KV_HEADS = N_HEADS // 4
C_GROUP = N_HEADS // C_KV_HEADS
C_WINDOW = 128
N_GROUPS = 4
EXPERTS_PER_GROUP = 8
N_EXPERTS = N_GROUPS * EXPERTS_PER_GROUP
EXPERT_TOP_K = 2
D_EXPERT = D_MODEL // 4
D_PLE = 256
NORM_EPS = 1e-6
NEG_INF = -1e30
SCALE = HEAD_DIM ** -0.5

LANES = 128
VMEM_LIMIT = 56 * 1024 * 1024
MOE_BLOCK = 256
F32 = jnp.float32
BF16 = jnp.bfloat16


def _params(*sem):
    return pltpu.CompilerParams(dimension_semantics=sem, vmem_limit_bytes=VMEM_LIMIT)


def _rmsnorm_kernel(x_ref, g_ref, o_ref):
    x = x_ref[...]
    ms = jnp.mean(x * x, axis=-1, keepdims=True)
    o_ref[...] = (x * lax.rsqrt(ms + NORM_EPS) * g_ref[...]).astype(o_ref.dtype)


def rmsnorm(x, g, out_dtype, row0=0, m=None, tm=256):
    d = x.shape[1]
    m = x.shape[0] if m is None else m
    b0 = row0 // tm
    return pl.pallas_call(
        _rmsnorm_kernel,
        grid=(m // tm,),
        in_specs=[pl.BlockSpec((tm, d), lambda i: (b0 + i, 0)), pl.BlockSpec((1, d), lambda i: (0, 0))],
        out_specs=pl.BlockSpec((tm, d), lambda i: (i, 0)),
        out_shape=jax.ShapeDtypeStruct((m, d), out_dtype),
        compiler_params=_params("parallel"),
        name="rmsnorm",
    )(x, g.reshape(1, d))


def _mm_kernel(x_ref, w_ref, o_ref):
    o_ref[...] = jnp.dot(x_ref[...], w_ref[...], preferred_element_type=F32).astype(o_ref.dtype)


def _layer_spec(block, index_map, layer):
    return pl.BlockSpec((None,) + tuple(block), lambda *a: (layer,) + tuple(index_map(*a)))


def matmul(x, w, layer, out_dtype, tm=1024, tn=1024):
    m, k = x.shape
    n = w.shape[2]
    return pl.pallas_call(
        _mm_kernel,
        grid=(m // tm, n // tn),
        in_specs=[pl.BlockSpec((tm, k), lambda i, j: (i, 0)), _layer_spec((k, tn), lambda i, j: (0, j), layer)],
        out_specs=pl.BlockSpec((tm, tn), lambda i, j: (i, j)),
        out_shape=jax.ShapeDtypeStruct((m, n), out_dtype),
        compiler_params=_params("parallel", "parallel"),
        name="matmul",
    )(x, w)


def _mm_res_kernel(xa_ref, xb_ref, w_ref, r_ref, o_ref, *, na):
    @pl.when(pl.program_id(0) < na)
    def _():
        o_ref[...] = r_ref[...] + jnp.dot(xa_ref[...], w_ref[...], preferred_element_type=F32)

    @pl.when(pl.program_id(0) >= na)
    def _():
        o_ref[...] = r_ref[...] + jnp.dot(xb_ref[...], w_ref[...], preferred_element_type=F32)


def matmul_residual(xa, xb, w, layer, res, tm=512, tn=1024):
    ma, k = xa.shape
    mb = xb.shape[0]
    n = w.shape[2]
    na = ma // tm
    return pl.pallas_call(
        functools.partial(_mm_res_kernel, na=na),
        grid=((ma + mb) // tm, n // tn),
        in_specs=[
            pl.BlockSpec((tm, k), lambda i, j: (jnp.minimum(i, na - 1), 0)),
            pl.BlockSpec((tm, k), lambda i, j: (jnp.maximum(i - na, 0), 0)),
            _layer_spec((k, tn), lambda i, j: (0, j), layer),
            pl.BlockSpec((tm, tn), lambda i, j: (i, j)),
        ],
        out_specs=pl.BlockSpec((tm, tn), lambda i, j: (i, j)),
        out_shape=jax.ShapeDtypeStruct((ma + mb, n), F32),
        compiler_params=_params("arbitrary", "arbitrary"),
        name="matmul_residual",
    )(xa, xb, w, res)


def _ple_kernel(x_ref, wg_ref, p_ref, wp_ref, r_ref, o_ref):
    gate_logit = jnp.dot(x_ref[...], wg_ref[...], preferred_element_type=F32)
    proj = jnp.dot(p_ref[...], wp_ref[...], preferred_element_type=F32)
    gate = 1.0 / (1.0 + jnp.exp(-gate_logit))
    o_ref[...] = r_ref[...] + gate * proj


def ple_update(xn, w_gate, p, w_proj, layer, res, tm=1024, tn=512):
    m, k = xn.shape
    n = w_gate.shape[2]
    kp = p.shape[2]
    return pl.pallas_call(
        _ple_kernel,
        grid=(m // tm, n // tn),
        in_specs=[
            pl.BlockSpec((tm, k), lambda i, j: (i, 0)),
            _layer_spec((k, tn), lambda i, j: (0, j), layer),
            _layer_spec((tm, kp), lambda i, j: (i, 0), layer),
            _layer_spec((kp, tn), lambda i, j: (0, j), layer),
            pl.BlockSpec((tm, tn), lambda i, j: (i, j)),
        ],
        out_specs=pl.BlockSpec((tm, tn), lambda i, j: (i, j)),
        out_shape=jax.ShapeDtypeStruct((m, n), F32),
        compiler_params=_params("parallel", "parallel"),
        name="ple_update",
    )(xn, w_gate, p, w_proj, res)


def _qk_scores(q, k):
    return lax.dot_general(q, k, (((1,), (1,)), ((), ())), preferred_element_type=F32)


NA_QROWS = 4
NA_KROWS = NA_QROWS + NA_ROWS
NA_VARIANTS = 3


def _na_kernel(q_ref, k_ref, v_ref, b_ref, o_ref, *, rows, unroll):
    nq = NA_QROWS * GRID_W
    nk = NA_KROWS * GRID_W
    nblk = rows // NA_QROWS

    def body(it, carry):
        for u in range(unroll):
            blk = it * unroll + u
            r0 = blk * NA_QROWS
            ks = jnp.clip(r0 - NA_ROWS // 2, 0, rows - NA_KROWS)
            variant = jnp.where(blk == 0, 0, jnp.where(blk == nblk - 1, 2, 1))
            q0 = pl.multiple_of(r0 * GRID_W, nq)
            k0 = pl.multiple_of(ks * GRID_W, GRID_W)
            q = q_ref[pl.ds(q0, nq), :]
            k = k_ref[pl.ds(k0, nk), :]
            v = v_ref[pl.ds(k0, nk), :]
            s = _qk_scores(q, k) * SCALE + b_ref[variant]
            m = jnp.max(s, axis=-1, keepdims=True)
            e = jnp.exp(s - m)
            l = jnp.sum(e, axis=-1, keepdims=True)
            o = jnp.dot(e.astype(BF16), v, preferred_element_type=F32) / l
            o_ref[pl.ds(q0, nq), :] = o.astype(o_ref.dtype)
        return carry

    lax.fori_loop(0, nblk // unroll, body, 0)


def _na_slab_tables():
    import numpy as np
    rowsel = np.zeros((NA_VARIANTS, NA_QROWS, NA_KROWS, 2 * NA_ROWS - 1), np.float32)
    half = NA_ROWS // 2
    for v in range(NA_VARIANTS):
        for i in range(NA_QROWS):
            k_first = (0, -half, NA_QROWS - NA_KROWS)[v]
            w_first = (0, i - half, NA_QROWS - NA_ROWS)[v]
            for a in range(NA_KROWS):
                if 0 <= k_first + a - w_first < NA_ROWS:
                    rowsel[v, i, a, k_first + a - i + NA_ROWS - 1] = 1.0
    colsel = np.zeros((GRID_W, GRID_W, 2 * NA_COLS - 1), np.float32)
    for qc in range(GRID_W):
        cstart = min(max(qc - NA_COLS // 2, 0), GRID_W - NA_COLS)
        for kc in range(cstart, cstart + NA_COLS):
            colsel[qc, kc, kc - qc + NA_COLS - 1] = 1.0
    return rowsel, colsel


def na_bias_slabs(rpb):
    rowsel, colsel = _na_slab_tables()
    hi = lax.Precision.HIGHEST
    t = jnp.einsum("hrc,viar->hviac", rpb, rowsel, precision=hi)
    t = jnp.einsum("hviac,qkc->hviqak", t, colsel, precision=hi)
    ok = (rowsel.sum(-1)[:, :, None, :, None] * colsel.sum(-1)[None, None, :, None, :]) > 0
    t = jnp.where(ok[None], t, NEG_INF)
    return t.reshape(rpb.shape[0], NA_VARIANTS, NA_QROWS * GRID_W, NA_KROWS * GRID_W)


def neighbourhood_attention(qkv, slabs, n_seq, seq_off, s):
    rows = s // GRID_W
    unroll = 4
    assert rows % (unroll * NA_QROWS) == 0 and rows >= NA_KROWS + NA_QROWS
    blk = lambda c0: pl.BlockSpec((s, HEAD_DIM), lambda b, h: (seq_off + b, c0 + h))
    return pl.pallas_call(
        functools.partial(_na_kernel, rows=rows, unroll=unroll),
        grid=(n_seq, N_HEADS),
        in_specs=[
            blk(0), blk(N_HEADS), blk(2 * N_HEADS),
            pl.BlockSpec((None, NA_VARIANTS, NA_QROWS * GRID_W, NA_KROWS * GRID_W), lambda b, h: (h, 0, 0, 0)),
        ],
        out_specs=pl.BlockSpec((s, HEAD_DIM), lambda b, h: (b, h)),
        out_shape=jax.ShapeDtypeStruct((n_seq * s, D_MODEL), BF16),
        compiler_params=_params("parallel", "parallel"),
        name="neighbourhood_attention",
    )(qkv, qkv, qkv, slabs)


DIL_QBLOCK = 256
DIL_FAR = 1e30


def _dil_geometry(s, window, dil):
    half = window // (2 * dil)
    sub = s // dil
    qb = min(sub, DIL_QBLOCK)
    kb = sub if sub <= DIL_QBLOCK else qb + 2 * half
    return half, sub, qb, kb, sub // qb


def _dil_branches(s):
    return sorted(DIL_PATTERNS, key=lambda p: -p[1])


def _dil_distance_tiles(s):
    import numpy as np
    tiles, keys, which = [], [], []
    for window, dil in _dil_branches(s):
        half, sub, qb, kb, nch = _dil_geometry(s, window, dil)
        key = (half, qb, kb, nch > 1)
        if key not in keys:
            keys.append(key)
            offsets = (0, half, kb - qb) if nch > 1 else (0,)
            rel = np.abs(np.arange(kb)[None, None, :] - np.arange(qb)[None, :, None]
                         - np.asarray(offsets)[:, None, None])
            tiles.append(np.where(rel <= half, rel, DIL_FAR).astype(np.float32))
        which.append(keys.index(key))
    return tiles, which


def _dil_kernel(slope_ref, q_ref, k_ref, v_ref, *rest, s, which):
    tile_refs = rest[:-4]
    o_ref, acc_ref, m_ref, l_ref = rest[-4:]
    slope = slope_ref[pl.program_id(1)]
    for bi, (window, dil) in enumerate(_dil_branches(s)):
        half, sub, qb, kb, nch = _dil_geometry(s, window, dil)
        stride = dil if dil > 1 else None
        t_ref = tile_refs[which[bi]]

        def chunk(it, dil=dil, half=half, sub=sub, qb=qb, kb=kb, nch=nch, stride=stride, t_ref=t_ref):
            res = it // nch
            c = it % nch
            u0 = c * qb
            ks = jnp.clip(u0 - half, 0, sub - kb)
            q_idx = pl.ds(res + u0 * dil, qb, stride=stride)
            k_idx = pl.ds(res + ks * dil, kb, stride=stride)
            q = (q_ref[q_idx, :] * SCALE).astype(BF16)
            k = k_ref[k_idx, :].astype(BF16)
            v = v_ref[k_idx, :].astype(BF16)
            dist = t_ref[jnp.where(c == 0, 0, jnp.where(c == nch - 1, 2, 1))] if nch > 1 else t_ref[0]
            sc = _qk_scores(q, k) - (slope * dil) * dist
            m_i = jnp.max(sc, axis=-1, keepdims=True)
            e = jnp.exp(sc - m_i)
            l_i = jnp.sum(e, axis=-1, keepdims=True)
            a_i = jnp.dot(e.astype(BF16), v, preferred_element_type=F32)
            return q_idx, a_i, jnp.broadcast_to(m_i, (qb, HEAD_DIM)), jnp.broadcast_to(l_i, (qb, HEAD_DIM))

        unroll = 8 if qb * kb <= 128 * 128 else 2

        def body(it, carry, bi=bi, chunk=chunk, unroll=unroll):
            for q_idx, a_i, m_b, l_b in [chunk(it * unroll + u) for u in range(unroll)]:
                if bi == 0:
                    acc_ref[q_idx, :] = a_i
                    m_ref[q_idx, :] = m_b
                    l_ref[q_idx, :] = l_b
                else:
                    m_o = m_ref[q_idx, :]
                    m_n = jnp.maximum(m_o, m_b)
                    w_o = jnp.exp(m_o - m_n)
                    w_i = jnp.exp(m_b - m_n)
                    acc_ref[q_idx, :] = w_o * acc_ref[q_idx, :] + w_i * a_i
                    l_ref[q_idx, :] = w_o * l_ref[q_idx, :] + w_i * l_b
                    m_ref[q_idx, :] = m_n
            return carry

        assert (dil * nch) % unroll == 0
        lax.fori_loop(0, dil * nch // unroll, body, 0)
    o_ref[...] = (acc_ref[...] / l_ref[...]).astype(o_ref.dtype)


def dilated_attention(qkv, slopes, n_seq, seq_off, s):
    blk = lambda c0: pl.BlockSpec((s, HEAD_DIM), lambda b, h: (seq_off + b, c0 + h))
    tiles, which = _dil_distance_tiles(s)
    tile_specs = [pl.BlockSpec(t.shape, lambda b, h: (0, 0, 0)) for t in tiles]
    return pl.pallas_call(
        functools.partial(_dil_kernel, s=s, which=tuple(which)),
        grid=(n_seq, N_HEADS),
        in_specs=[pl.BlockSpec(memory_space=pltpu.SMEM), blk(0), blk(N_HEADS), blk(2 * N_HEADS)] + tile_specs,
        out_specs=pl.BlockSpec((s, HEAD_DIM), lambda b, h: (b, h)),
        out_shape=jax.ShapeDtypeStruct((n_seq * s, D_MODEL), BF16),
        scratch_shapes=[pltpu.VMEM((s, HEAD_DIM), F32)] * 3,
        compiler_params=_params("parallel", "parallel"),
        name="dilated_attention",
    )(slopes, qkv, qkv, qkv, *[jnp.asarray(t) for t in tiles])


def _gqa_kernel(slope_ref, sink_ref, q_ref, k_ref, v_ref, o_ref, *, s, qb):
    kvh = pl.program_id(1)
    kb = qb + 2 * C_WINDOW

    def body(n, carry):
        q0 = pl.multiple_of(n * qb, qb)
        ks = pl.multiple_of(jnp.clip(n * qb - C_WINDOW, 0, s - kb), C_WINDOW)
        k = k_ref[pl.ds(ks, kb), :]
        v = v_ref[pl.ds(ks, kb), :]
        tq = q0 + lax.broadcasted_iota(jnp.int32, (qb, kb), 0)
        tk = ks + lax.broadcasted_iota(jnp.int32, (qb, kb), 1)
        rel = jnp.abs(tk - tq)
        valid = rel <= C_WINDOW
        relf = rel.astype(F32)
        for g in range(C_GROUP):
            head = kvh * C_GROUP + g
            slope = slope_ref[head]
            sink = sink_ref[head]
            q = q_ref[pl.ds(q0, qb), g * HEAD_DIM:(g + 1) * HEAD_DIM]
            sc = _qk_scores(q, k) * SCALE - slope * relf
            sc = jnp.where(valid, sc, NEG_INF)
            m = jnp.maximum(jnp.max(sc, axis=-1, keepdims=True), sink)
            e = jnp.exp(sc - m)
            l = jnp.sum(e, axis=-1, keepdims=True) + jnp.exp(sink - m)
            o = jnp.dot(e.astype(BF16), v, preferred_element_type=F32) / l
            o_ref[pl.ds(q0, qb), g * HEAD_DIM:(g + 1) * HEAD_DIM] = o.astype(o_ref.dtype)
        return carry

    lax.fori_loop(0, s // qb, body, 0)


def gqa_sink_attention(qkv, slopes, sink, n_seq, seq_off, s):
    gw = C_GROUP * HEAD_DIM
    kv0 = D_MODEL // HEAD_DIM
    smem = pl.BlockSpec(memory_space=pltpu.SMEM)
    return pl.pallas_call(
        functools.partial(_gqa_kernel, s=s, qb=256),
        grid=(n_seq, C_KV_HEADS),
        in_specs=[
            smem, smem,
            pl.BlockSpec((s, gw), lambda b, h: (seq_off + b, h)),
            pl.BlockSpec((s, HEAD_DIM), lambda b, h: (seq_off + b, kv0 + h)),
            pl.BlockSpec((s, HEAD_DIM), lambda b, h: (seq_off + b, kv0 + C_KV_HEADS + h)),
        ],
        out_specs=pl.BlockSpec((s, gw), lambda b, h: (b, h)),
        out_shape=jax.ShapeDtypeStruct((n_seq * s, D_MODEL), BF16),
        compiler_params=_params("parallel", "parallel"),
        name="gqa_sink_attention",
    )(slopes, sink, qkv, qkv, qkv)


def _pack_bf16_pairs(x):
    half = x.shape[1] // 2
    hi = lax.bitcast_convert_type(x[:, :half].astype(BF16).astype(F32), jnp.uint32)
    lo = lax.bitcast_convert_type(x[:, half:].astype(BF16).astype(F32), jnp.uint32)
    return hi | (lo >> 16)


def _unpack_bf16_pairs(w):
    hi = lax.bitcast_convert_type(w & jnp.uint32(0xFFFF0000), F32)
    lo = lax.bitcast_convert_type(w << 16, F32)
    return hi, lo


def _route_kernel(h_ref, g_ref, wh_ref, wl_ref, b_ref, xp_ref, r_ref):
    x = h_ref[...]
    ms = jnp.mean(x * x, axis=-1, keepdims=True)
    xn = x * lax.rsqrt(ms + NORM_EPS) * g_ref[...]
    xp_ref[...] = _pack_bf16_pairs(xn)
    xh = xn.astype(BF16)
    xl = (xn - xh.astype(F32)).astype(BF16)
    wh = wh_ref[...]
    logits = jnp.dot(xh, wh, preferred_element_type=F32) + (
        jnp.dot(xl, wh, preferred_element_type=F32) + jnp.dot(xh, wl_ref[...], preferred_element_type=F32))
    logits = logits + b_ref[...]
    lane = lax.broadcasted_iota(jnp.int32, logits.shape, 1).astype(F32)
    rmax = lambda a: jnp.max(a, axis=-1, keepdims=True)
    rmin = lambda a: jnp.min(a, axis=-1, keepdims=True)
    rsum = lambda a: jnp.sum(a, axis=-1, keepdims=True)

    gmask = lane < N_GROUPS
    gl = jnp.where(gmask, logits, NEG_INF)
    ge = jnp.where(gmask, jnp.exp(gl - rmax(gl)), 0.0)
    gp = jnp.where(gmask, ge / rsum(ge), -1.0)
    g_p = rmax(gp)
    g_idx = rmin(jnp.where(gp == g_p, lane, float(LANES)))

    lo = N_GROUPS + EXPERTS_PER_GROUP * g_idx
    emask = (lane >= lo) & (lane < lo + EXPERTS_PER_GROUP)
    el = jnp.where(emask, logits, NEG_INF)
    ee = jnp.where(emask, jnp.exp(el - rmax(el)), 0.0)
    ep = jnp.where(emask, ee / rsum(ee), -1.0)
    p1 = rmax(ep)
    i1 = rmin(jnp.where(ep == p1, lane, float(LANES)))
    ep2 = jnp.where(lane == i1, -1.0, ep)
    p2 = rmax(ep2)
    i2 = rmin(jnp.where(ep2 == p2, lane, float(LANES)))
    den = p1 + p2
    out = jnp.where(lane == 0, g_p * (p1 / den), 0.0)
    out = jnp.where(lane == 1, g_p * (p2 / den), out)
    out = jnp.where(lane == 2, i1 - N_GROUPS, out)
    out = jnp.where(lane == 3, i2 - N_GROUPS, out)
    r_ref[...] = out


def route_weights(w_group, b_group, w_expert, b_expert):
    n_pad = LANES - N_GROUPS - N_EXPERTS
    w = jnp.pad(jnp.concatenate([w_group, w_expert], axis=-1), ((0, 0), (0, 0), (0, n_pad)))
    b = jnp.pad(jnp.concatenate([b_group, b_expert], axis=-1), ((0, 0), (0, n_pad)))[:, None, :]
    wh = w.astype(BF16)
    wl = (w - wh.astype(F32)).astype(BF16)
    return wh, wl, b


def moe_route(h, g, wh, wl, b, layer, tm=256):
    t, d = h.shape
    return pl.pallas_call(
        _route_kernel,
        grid=(t // tm,),
        in_specs=[
            pl.BlockSpec((tm, d), lambda i: (i, 0)),
            pl.BlockSpec((1, d), lambda i: (0, 0)),
            _layer_spec((d, LANES), lambda i: (0, 0), layer),
            _layer_spec((d, LANES), lambda i: (0, 0), layer),
            _layer_spec((1, LANES), lambda i: (0, 0), layer),
        ],
        out_specs=[pl.BlockSpec((tm, d // 2), lambda i: (i, 0)), pl.BlockSpec((tm, LANES), lambda i: (i, 0))],
        out_shape=[jax.ShapeDtypeStruct((t, d // 2), jnp.uint32), jax.ShapeDtypeStruct((t, LANES), F32)],
        compiler_params=_params("parallel"),
        name="moe_route",
    )(h, g.reshape(1, d), wh, wl, b)


def moe_plan(expert_ids):
    t = expert_ids.shape[0]
    a = t * EXPERT_TOP_K
    n_blocks = a // MOE_BLOCK + N_EXPERTS
    cap = n_blocks * MOE_BLOCK
    flat_e = expert_ids.reshape(a)
    order = jnp.argsort(flat_e, stable=True).astype(jnp.int32)
    rank = jnp.argsort(order).astype(jnp.int32)
    counts = jnp.sum((flat_e[:, None] == jnp.arange(N_EXPERTS, dtype=jnp.int32)[None, :]).astype(jnp.int32), axis=0)
    padded = (counts + MOE_BLOCK - 1) // MOE_BLOCK * MOE_BLOCK
    start = jnp.cumsum(counts) - counts
    pend = jnp.cumsum(padded)
    pstart = pend - padded
    pos = (pstart[flat_e] + rank - start[flat_e]).astype(jnp.int32)
    block_first = jnp.arange(n_blocks, dtype=jnp.int32) * MOE_BLOCK
    block_e = jnp.minimum(jnp.sum((pend[None, :] <= block_first[:, None]).astype(jnp.int32), axis=1), N_EXPERTS - 1)
    slot = jnp.arange(cap, dtype=jnp.int32)
    slot_e = jnp.repeat(block_e, MOE_BLOCK)
    slot_rank = slot - pstart[slot_e]
    sorted_pos = jnp.clip(start[slot_e] + slot_rank, 0, a - 1)
    slot_tok = jnp.where(slot_rank < counts[slot_e], order[sorted_pos] // EXPERT_TOP_K, 0).astype(jnp.int32)
    n_used = (pend[N_EXPERTS - 1:] // MOE_BLOCK).astype(jnp.int32)
    return slot_tok, pos, block_e.astype(jnp.int32), n_used


def _row_copy(src_ref, dst_ref, sem, src_row, dst_row):
    return pltpu.make_async_copy(src_ref.at[pl.ds(src_row, 1), :], dst_ref.at[pl.ds(dst_row, 1), :], sem)


GATHER_UNROLL = 8


def _start_row_gather(src_hbm, idx_ref, idx0, idx_stride, dst, sem, inline=False):
    n = dst.shape[0]
    if inline:
        for r in range(n):
            _row_copy(src_hbm, dst, sem, idx_ref[idx0 + r * idx_stride], r).start()
        return

    def issue(it, carry):
        for u in range(GATHER_UNROLL):
            r = it * GATHER_UNROLL + u
            _row_copy(src_hbm, dst, sem, idx_ref[idx0 + r * idx_stride], r).start()
        return carry

    lax.fori_loop(0, n // GATHER_UNROLL, issue, 0)


def _wait_row_gather(src_hbm, dst, sem):
    pltpu.make_async_copy(src_hbm.at[pl.ds(0, dst.shape[0]), :], dst, sem).wait()


def _prefetched_gather(step, n_steps, start, wait):
    @pl.when(step == 0)
    def _():
        start(0, 0, False)

    slot = step % 2
    wait(slot)
    start(jnp.minimum(step + 1, n_steps - 1), 1 - slot, True)
    return slot


def _drain_gather(step, n_steps, wait):
    @pl.when(step == n_steps - 1)
    def _():
        wait(1 - step % 2)


def _moe_up_kernel(tok_ref, be_ref, nu_ref, x_hbm, wg_ref, wu_ref, o_ref, buf, sems):
    rb = pl.program_id(0)
    n_used = nu_ref[0]

    def start(block, slot, inline):
        _start_row_gather(x_hbm, tok_ref, block * MOE_BLOCK, 1, buf.at[slot], sems.at[slot], inline)

    def wait(slot):
        _wait_row_gather(x_hbm, buf.at[slot], sems.at[slot])

    @pl.when(rb < n_used)
    def _():
        slot = _prefetched_gather(rb, n_used, start, wait)
        x_hi, x_lo = _unpack_bf16_pairs(buf[slot])
        x_hi = x_hi.astype(BF16)
        x_lo = x_lo.astype(BF16)
        half = x_hi.shape[1]

        def proj(w_ref):
            return (jnp.dot(x_hi, w_ref[:half, :], preferred_element_type=F32)
                    + jnp.dot(x_lo, w_ref[half:, :], preferred_element_type=F32))

        g = proj(wg_ref)
        u = proj(wu_ref)
        o_ref[...] = (g * (1.0 / (1.0 + jnp.exp(-g))) * u).astype(o_ref.dtype)
        _drain_gather(rb, n_used, wait)

    @pl.when(rb >= n_used)
    def _():
        o_ref[...] = jnp.zeros_like(o_ref)


def moe_up(xp, slot_tok, w_gate, w_up, layer, block_e, n_used):
    d = w_gate.shape[2]
    n = w_gate.shape[3]
    cap = slot_tok.shape[0]
    w_spec = pl.BlockSpec((None, None, d, n), lambda rb, tok, be, nu: (layer, be[rb], 0, 0))
    return pl.pallas_call(
        _moe_up_kernel,
        grid_spec=pltpu.PrefetchScalarGridSpec(
            num_scalar_prefetch=3,
            grid=(cap // MOE_BLOCK,),
            in_specs=[pl.BlockSpec(memory_space=pl.ANY), w_spec, w_spec],
            out_specs=pl.BlockSpec((MOE_BLOCK, n), lambda rb, tok, be, nu: (rb, 0)),
            scratch_shapes=[pltpu.VMEM((2, MOE_BLOCK, d // 2), jnp.uint32), pltpu.SemaphoreType.DMA((2,))],
        ),
        out_shape=jax.ShapeDtypeStruct((cap, n), BF16),
        compiler_params=_params("arbitrary"),
        name="moe_up",
    )(slot_tok, block_e, n_used, xp, w_gate, w_up)


def _moe_down_kernel(be_ref, nu_ref, x_ref, w_ref, o_ref, wb_ref):
    rb = pl.program_id(0)
    n_used = nu_ref[0]

    @pl.when(rb < n_used)
    def _():
        @pl.when((rb == 0) | (be_ref[rb] != be_ref[jnp.maximum(rb - 1, 0)]))
        def _():
            wb_ref[...] = w_ref[...].astype(BF16)

        o_ref[...] = _pack_bf16_pairs(jnp.dot(x_ref[...], wb_ref[...], preferred_element_type=F32))

    @pl.when(rb >= n_used)
    def _():
        o_ref[...] = jnp.zeros_like(o_ref)


def moe_down(hs, w_down, layer, block_e, n_used):
    cap, k = hs.shape
    n = w_down.shape[3]
    return pl.pallas_call(
        _moe_down_kernel,
        grid_spec=pltpu.PrefetchScalarGridSpec(
            num_scalar_prefetch=2,
            grid=(cap // MOE_BLOCK,),
            in_specs=[
                pl.BlockSpec((MOE_BLOCK, k), lambda rb, be, nu: (rb, 0)),
                pl.BlockSpec((None, None, k, n), lambda rb, be, nu: (layer, be[rb], 0, 0)),
            ],
            out_specs=pl.BlockSpec((MOE_BLOCK, n // 2), lambda rb, be, nu: (rb, 0)),
            scratch_shapes=[pltpu.VMEM((k, n), BF16)],
        ),
        out_shape=jax.ShapeDtypeStruct((cap, n // 2), jnp.uint32),
        compiler_params=_params("arbitrary"),
        name="moe_down",
    )(block_e, n_used, hs, w_down)


def _combine_kernel(pos_ref, h_ref, g_ref, gain_ref, y_hbm, o_ref, xn_ref, buf, sems, *, tm):
    i = pl.program_id(0)
    n_steps = pl.num_programs(0)

    def start(block, slot, inline):
        for k in range(EXPERT_TOP_K):
            _start_row_gather(y_hbm, pos_ref, block * tm * EXPERT_TOP_K + k, EXPERT_TOP_K,
                              buf.at[slot, k], sems.at[slot, k], inline)

    def wait(slot):
        for k in range(EXPERT_TOP_K):
            _wait_row_gather(y_hbm, buf.at[slot, k], sems.at[slot, k])

    slot = _prefetched_gather(i, n_steps, start, wait)
    g = g_ref[...]
    half = h_ref.shape[1] // 2
    y0 = _unpack_bf16_pairs(buf[slot, 0])
    y1 = _unpack_bf16_pairs(buf[slot, 1])
    hn = [h_ref[:, c * half:(c + 1) * half] + (g[:, 0:1] * y0[c] + g[:, 1:2] * y1[c]) for c in range(2)]
    ssq = sum(jnp.sum(x * x, axis=-1, keepdims=True) for x in hn)
    r = lax.rsqrt(ssq / h_ref.shape[1] + NORM_EPS)
    for c in range(2):
        cols = slice(c * half, (c + 1) * half)
        o_ref[:, cols] = hn[c]
        xn_ref[:, cols] = (hn[c] * r * gain_ref[:, cols]).astype(xn_ref.dtype)
    _drain_gather(i, n_steps, wait)


def moe_combine(h, route, ys, pos, next_gain, tm=256):
    t, d = h.shape
    row = lambda i, pos: (i, 0)
    return pl.pallas_call(
        functools.partial(_combine_kernel, tm=tm),
        grid_spec=pltpu.PrefetchScalarGridSpec(
            num_scalar_prefetch=1,
            grid=(t // tm,),
            in_specs=[
                pl.BlockSpec((tm, d), row),
                pl.BlockSpec((tm, LANES), row),
                pl.BlockSpec((1, d), lambda i, pos: (0, 0)),
                pl.BlockSpec(memory_space=pl.ANY),
            ],
            out_specs=[pl.BlockSpec((tm, d), row), pl.BlockSpec((tm, d), row)],
            scratch_shapes=[pltpu.VMEM((2, EXPERT_TOP_K, tm, d // 2), jnp.uint32),
                            pltpu.SemaphoreType.DMA((2, EXPERT_TOP_K))],
        ),
        out_shape=[jax.ShapeDtypeStruct((t, d), F32), jax.ShapeDtypeStruct((t, d), BF16)],
        compiler_params=_params("arbitrary"),
        name="moe_combine",
    )(pos, h, route, next_gain.reshape(1, d), ys)


def hierarchical_moe_update(h, g, route_wh, route_wl, route_b, w_gate, w_up, w_down, layer, next_gain):
    xp, route = moe_route(h, g, route_wh, route_wl, route_b, layer)
    expert_ids = route[:, 2:4].astype(jnp.int32)
    slot_tok, pos, block_e, n_used = moe_plan(expert_ids)
    hs = moe_up(xp, slot_tok, w_gate, w_up, layer, block_e, n_used)
    ys = moe_down(hs, w_down, layer, block_e, n_used)
    return moe_combine(h, route, ys, pos, next_gain)


def kernel(x_prompt, x_sample, p_prompt, p_sample, norm_mix, norm_ffn, a_w_qkv, a_rpb, a_w_o, b_w_qkv, b_w_o,
           c_w_qkv, c_sink, c_w_o, moe_w_group, moe_b_group, moe_w_expert, moe_b_expert, moe_w_gate, moe_w_up,
           moe_w_down, ple_w_proj, ple_norm, ple_w_gate, final_norm):
    b1, s1, d = x_prompt.shape
    b2, s2, _ = x_sample.shape
    t1 = b1 * s1
    t2 = b2 * s2
    assert t1 % s2 == 0
    seqs = ((b1, 0, s1), (b2, t1 // s2, s2))

    h = jnp.concatenate([x_prompt.reshape(t1, d), x_sample.reshape(t2, d)], axis=0)
    p = jnp.concatenate([p_prompt.reshape(DEPTH, t1, D_PLE), p_sample.reshape(DEPTH, t2, D_PLE)], axis=1).astype(BF16)
    slopes = 2.0 ** (-8.0 * jnp.arange(1, N_HEADS + 1, dtype=F32) / N_HEADS)
    route_wh, route_wl, route_b = route_weights(moe_w_group, moe_b_group, moe_w_expert, moe_b_expert)

    w_qkv = [w.astype(BF16) for w in (a_w_qkv, b_w_qkv, c_w_qkv)]
    w_out = [w.astype(BF16) for w in (a_w_o, b_w_o, c_w_o)]
    w_gate = moe_w_gate.astype(BF16)
    w_up = moe_w_up.astype(BF16)
    w_ple_gate = ple_w_gate.astype(BF16)
    w_ple_proj = ple_w_proj.astype(BF16)

    for i in range(DEPTH):
        kind = i % N_MIXERS
        j = i // N_MIXERS
        xn = rmsnorm(h, norm_mix[i], BF16)
        if kind == 0:
            qkv = matmul(xn, w_qkv[kind], j, BF16)
            slabs = na_bias_slabs(a_rpb[j])
            o = [neighbourhood_attention(qkv, slabs, n, off, s) for n, off, s in seqs]
        elif kind == 1:
            qkv = matmul(xn, w_qkv[kind], j, F32)
            o = [dilated_attention(qkv, slopes, n, off, s) for n, off, s in seqs]
        else:
            qkv = matmul(xn, w_qkv[kind], j, BF16)
            o = [gqa_sink_attention(qkv, slopes, c_sink[j], n, off, s) for n, off, s in seqs]
        h = matmul_residual(o[0], o[1], w_out[kind], j, h)
        h, xn = hierarchical_moe_update(
            h, norm_ffn[i], route_wh, route_wl, route_b, w_gate, w_up, moe_w_down, i, ple_norm[i])
        h = ple_update(xn, w_ple_gate, p, w_ple_proj, i, h)

    y1 = rmsnorm(h, final_norm, F32, 0, t1)
    y2 = rmsnorm(h, final_norm, F32, t1, t2)
    return y1.reshape(b1, s1, d), y2.reshape(b2, s2, d)
```

```python
import functools

import jax
import jax.numpy as jnp
from jax import lax
from jax.experimental import pallas as pl
from jax.experimental.pallas import tpu as pltpu

D_MODEL = 4096
HEAD_DIM = 128
N_HEADS = D_MODEL // HEAD_DIM
DEPTH = 4
N_MIXERS = 3
GRID_W = 64
NA_ROWS = 8
NA_COLS = 16
DIL_PATTERNS = ((128, 1), (512, 4), (2048, 16))
C_KV_HEADS = N_HEADS // 4
C_GROUP = N_HEADS // C_KV_HEADS
C_WINDOW = 128
N_GROUPS = 4
EXPERTS_PER_GROUP = 8
N_EXPERTS = N_GROUPS * EXPERTS_PER_GROUP
EXPERT_TOP_K = 2
D_EXPERT = D_MODEL // 4
D_PLE = 256
NORM_EPS = 1e-6
NEG_INF = -1e30
SCALE = HEAD_DIM ** -0.5

LANES = 128
VMEM_LIMIT = 56 * 1024 * 1024
MOE_BLOCK = 256
F32 = jnp.float32
BF16 = jnp.bfloat16


def _params(*sem):
    return pltpu.CompilerParams(dimension_semantics=sem, vmem_limit_bytes=VMEM_LIMIT)


def _rmsnorm_kernel(x_ref, g_ref, o_ref):
    x = x_ref[...]
    ms = jnp.mean(x * x, axis=-1, keepdims=True)
    o_ref[...] = (x * lax.rsqrt(ms + NORM_EPS) * g_ref[...]).astype(o_ref.dtype)


def rmsnorm(x, g, out_dtype, row0=0, m=None, tm=256):
    d = x.shape[1]
    m = x.shape[0] if m is None else m
    b0 = row0 // tm
    return pl.pallas_call(
        _rmsnorm_kernel,
        grid=(m // tm,),
        in_specs=[pl.BlockSpec((tm, d), lambda i: (b0 + i, 0)), pl.BlockSpec((1, d), lambda i: (0, 0))],
        out_specs=pl.BlockSpec((tm, d), lambda i: (i, 0)),
        out_shape=jax.ShapeDtypeStruct((m, d), out_dtype),
        compiler_params=_params("parallel"),
        name="rmsnorm",
    )(x, g.reshape(1, d))


def _mm_kernel(x_ref, w_ref, o_ref):
    o_ref[...] = jnp.dot(x_ref[...], w_ref[...], preferred_element_type=F32).astype(o_ref.dtype)


def _layer_spec(block, index_map, layer):
    return pl.BlockSpec((None,) + tuple(block), lambda *a: (layer,) + tuple(index_map(*a)))


def matmul(x, w, layer, out_dtype, tm=1024, tn=1024):
    m, k = x.shape
    n = w.shape[2]
    return pl.pallas_call(
        _mm_kernel,
        grid=(m // tm, n // tn),
        in_specs=[pl.BlockSpec((tm, k), lambda i, j: (i, 0)), _layer_spec((k, tn), lambda i, j: (0, j), layer)],
        out_specs=pl.BlockSpec((tm, tn), lambda i, j: (i, j)),
        out_shape=jax.ShapeDtypeStruct((m, n), out_dtype),
        compiler_params=_params("parallel", "parallel"),
        name="matmul",
    )(x, w)


def _mm_res_kernel(xa_ref, xb_ref, w_ref, r_ref, o_ref, *, na):
    @pl.when(pl.program_id(0) < na)
    def _():
        o_ref[...] = r_ref[...] + jnp.dot(xa_ref[...], w_ref[...], preferred_element_type=F32)

    @pl.when(pl.program_id(0) >= na)
    def _():
        o_ref[...] = r_ref[...] + jnp.dot(xb_ref[...], w_ref[...], preferred_element_type=F32)


def matmul_residual(xa, xb, w, layer, res, tm=512, tn=1024):
    ma, k = xa.shape
    mb = xb.shape[0]
    n = w.shape[2]
    na = ma // tm
    return pl.pallas_call(
        functools.partial(_mm_res_kernel, na=na),
        grid=((ma + mb) // tm, n // tn),
        in_specs=[
            pl.BlockSpec((tm, k), lambda i, j: (jnp.minimum(i, na - 1), 0)),
            pl.BlockSpec((tm, k), lambda i, j: (jnp.maximum(i - na, 0), 0)),
            _layer_spec((k, tn), lambda i, j: (0, j), layer),
            pl.BlockSpec((tm, tn), lambda i, j: (i, j)),
        ],
        out_specs=pl.BlockSpec((tm, tn), lambda i, j: (i, j)),
        out_shape=jax.ShapeDtypeStruct((ma + mb, n), F32),
        compiler_params=_params("arbitrary", "arbitrary"),
        name="matmul_residual",
    )(xa, xb, w, res)


def _ple_kernel(x_ref, wg_ref, p_ref, wp_ref, r_ref, o_ref):
    gate_logit = jnp.dot(x_ref[...], wg_ref[...], preferred_element_type=F32)
    proj = jnp.dot(p_ref[...], wp_ref[...], preferred_element_type=F32)
    gate = 1.0 / (1.0 + jnp.exp(-gate_logit))
    o_ref[...] = r_ref[...] + gate * proj


def ple_update(xn, w_gate, p, w_proj, layer, res, tm=1024, tn=512):
    m, k = xn.shape
    n = w_gate.shape[2]
    kp = p.shape[2]
    return pl.pallas_call(
        _ple_kernel,
        grid=(m // tm, n // tn),
        in_specs=[
            pl.BlockSpec((tm, k), lambda i, j: (i, 0)),
            _layer_spec((k, tn), lambda i, j: (0, j), layer),
            _layer_spec((tm, kp), lambda i, j: (i, 0), layer),
            _layer_spec((kp, tn), lambda i, j: (0, j), layer),
            pl.BlockSpec((tm, tn), lambda i, j: (i, j)),
        ],
        out_specs=pl.BlockSpec((tm, tn), lambda i, j: (i, j)),
        out_shape=jax.ShapeDtypeStruct((m, n), F32),
        compiler_params=_params("parallel", "parallel"),
        name="ple_update",
    )(xn, w_gate, p, w_proj, res)


def _qk_scores(q, k):
    return lax.dot_general(q, k, (((1,), (1,)), ((), ())), preferred_element_type=F32)


NA_QROWS = 4
NA_KROWS = NA_QROWS + NA_ROWS
NA_VARIANTS = 3


def _na_kernel(q_ref, k_ref, v_ref, b_ref, o_ref, *, rows, unroll):
    nq = NA_QROWS * GRID_W
    nk = NA_KROWS * GRID_W
    nblk = rows // NA_QROWS

    def body(it, carry):
        for u in range(unroll):
            blk = it * unroll + u
            r0 = blk * NA_QROWS
            ks = jnp.clip(r0 - NA_ROWS // 2, 0, rows - NA_KROWS)
            variant = jnp.where(blk == 0, 0, jnp.where(blk == nblk - 1, 2, 1))
            q0 = pl.multiple_of(r0 * GRID_W, nq)
            k0 = pl.multiple_of(ks * GRID_W, GRID_W)
            q = q_ref[pl.ds(q0, nq), :]
            k = k_ref[pl.ds(k0, nk), :]
            v = v_ref[pl.ds(k0, nk), :]
            s = _qk_scores(q, k) * SCALE + b_ref[variant]
            m = jnp.max(s, axis=-1, keepdims=True)
            e = jnp.exp(s - m)
            l = jnp.sum(e, axis=-1, keepdims=True)
            o = jnp.dot(e.astype(BF16), v, preferred_element_type=F32) / l
            o_ref[pl.ds(q0, nq), :] = o.astype(o_ref.dtype)
        return carry

    lax.fori_loop(0, nblk // unroll, body, 0)


def _na_slab_tables():
    import numpy as np
    rowsel = np.zeros((NA_VARIANTS, NA_QROWS, NA_KROWS, 2 * NA_ROWS - 1), np.float32)
    half = NA_ROWS // 2
    for v in range(NA_VARIANTS):
        for i in range(NA_QROWS):
            k_first = (0, -half, NA_QROWS - NA_KROWS)[v]
            w_first = (0, i - half, NA_QROWS - NA_ROWS)[v]
            for a in range(NA_KROWS):
                if 0 <= k_first + a - w_first < NA_ROWS:
                    rowsel[v, i, a, k_first + a - i + NA_ROWS - 1] = 1.0
    colsel = np.zeros((GRID_W, GRID_W, 2 * NA_COLS - 1), np.float32)
    for qc in range(GRID_W):
        cstart = min(max(qc - NA_COLS // 2, 0), GRID_W - NA_COLS)
        for kc in range(cstart, cstart + NA_COLS):
            colsel[qc, kc, kc - qc + NA_COLS - 1] = 1.0
    return rowsel, colsel


def na_bias_slabs(rpb):
    rowsel, colsel = _na_slab_tables()
    hi = lax.Precision.HIGHEST
    t = jnp.einsum("hrc,viar->hviac", rpb, rowsel, precision=hi)
    t = jnp.einsum("hviac,qkc->hviqak", t, colsel, precision=hi)
    ok = (rowsel.sum(-1)[:, :, None, :, None] * colsel.sum(-1)[None, None, :, None, :]) > 0
    t = jnp.where(ok[None], t, NEG_INF)
    return t.reshape(rpb.shape[0], NA_VARIANTS, NA_QROWS * GRID_W, NA_KROWS * GRID_W)


def neighbourhood_attention(qkv, slabs, n_seq, seq_off, s):
    rows = s // GRID_W
    unroll = 4
    assert rows % (unroll * NA_QROWS) == 0 and rows >= NA_KROWS + NA_QROWS
    blk = lambda c0: pl.BlockSpec((s, HEAD_DIM), lambda b, h: (seq_off + b, c0 + h))
    return pl.pallas_call(
        functools.partial(_na_kernel, rows=rows, unroll=unroll),
        grid=(n_seq, N_HEADS),
        in_specs=[
            blk(0), blk(N_HEADS), blk(2 * N_HEADS),
            pl.BlockSpec((None, NA_VARIANTS, NA_QROWS * GRID_W, NA_KROWS * GRID_W), lambda b, h: (h, 0, 0, 0)),
        ],
        out_specs=pl.BlockSpec((s, HEAD_DIM), lambda b, h: (b, h)),
        out_shape=jax.ShapeDtypeStruct((n_seq * s, D_MODEL), BF16),
        compiler_params=_params("parallel", "parallel"),
        name="neighbourhood_attention",
    )(qkv, qkv, qkv, slabs)


DIL_QBLOCK = 256
DIL_FAR = 1e30


def _dil_geometry(s, window, dil):
    half = window // (2 * dil)
    sub = s // dil
    qb = min(sub, DIL_QBLOCK)
    kb = sub if sub <= DIL_QBLOCK else qb + 2 * half
    return half, sub, qb, kb, sub // qb


def _dil_branches(s):
    return sorted(DIL_PATTERNS, key=lambda p: -p[1])


def _dil_distance_tiles(s):
    import numpy as np
    tiles, keys, which = [], [], []
    for window, dil in _dil_branches(s):
        half, sub, qb, kb, nch = _dil_geometry(s, window, dil)
        key = (half, qb, kb, nch > 1)
        if key not in keys:
            keys.append(key)
            offsets = (0, half, kb - qb) if nch > 1 else (0,)
            rel = np.abs(np.arange(kb)[None, None, :] - np.arange(qb)[None, :, None]
                         - np.asarray(offsets)[:, None, None])
            tiles.append(np.where(rel <= half, rel, DIL_FAR).astype(np.float32))
        which.append(keys.index(key))
    return tiles, which


def _dil_kernel(slope_ref, q_ref, k_ref, v_ref, *rest, s, which):
    tile_refs = rest[:-4]
    o_ref, acc_ref, m_ref, l_ref = rest[-4:]
    slope = slope_ref[pl.program_id(1)]
    for bi, (window, dil) in enumerate(_dil_branches(s)):
        half, sub, qb, kb, nch = _dil_geometry(s, window, dil)
        stride = dil if dil > 1 else None
        t_ref = tile_refs[which[bi]]

        def chunk(it, dil=dil, half=half, sub=sub, qb=qb, kb=kb, nch=nch, stride=stride, t_ref=t_ref):
            res = it // nch
            c = it % nch
            u0 = c * qb
            ks = jnp.clip(u0 - half, 0, sub - kb)
            q_idx = pl.ds(res + u0 * dil, qb, stride=stride)
            k_idx = pl.ds(res + ks * dil, kb, stride=stride)
            q = (q_ref[q_idx, :] * SCALE).astype(BF16)
            k = k_ref[k_idx, :].astype(BF16)
            v = v_ref[k_idx, :].astype(BF16)
            dist = t_ref[jnp.where(c == 0, 0, jnp.where(c == nch - 1, 2, 1))] if nch > 1 else t_ref[0]
            sc = _qk_scores(q, k) - (slope * dil) * dist
            m_i = jnp.max(sc, axis=-1, keepdims=True)
            e = jnp.exp(sc - m_i)
            l_i = jnp.sum(e, axis=-1, keepdims=True)
            a_i = jnp.dot(e.astype(BF16), v, preferred_element_type=F32)
            return q_idx, a_i, jnp.broadcast_to(m_i, (qb, HEAD_DIM)), jnp.broadcast_to(l_i, (qb, HEAD_DIM))

        unroll = 8 if qb * kb <= 128 * 128 else 4

        def body(it, carry, bi=bi, chunk=chunk, unroll=unroll):
            for q_idx, a_i, m_b, l_b in [chunk(it * unroll + u) for u in range(unroll)]:
                if bi == 0:
                    acc_ref[q_idx, :] = a_i
                    m_ref[q_idx, :] = m_b
                    l_ref[q_idx, :] = l_b
                else:
                    m_o = m_ref[q_idx, :]
                    m_n = jnp.maximum(m_o, m_b)
                    w_o = jnp.exp(m_o - m_n)
                    w_i = jnp.exp(m_b - m_n)
                    acc_ref[q_idx, :] = w_o * acc_ref[q_idx, :] + w_i * a_i
                    l_ref[q_idx, :] = w_o * l_ref[q_idx, :] + w_i * l_b
                    m_ref[q_idx, :] = m_n
            return carry

        assert (dil * nch) % unroll == 0
        lax.fori_loop(0, dil * nch // unroll, body, 0)
    o_ref[...] = (acc_ref[...] / l_ref[...]).astype(o_ref.dtype)


def dilated_attention(qkv, slopes, n_seq, seq_off, s):
    blk = lambda c0: pl.BlockSpec((s, HEAD_DIM), lambda b, h: (seq_off + b, c0 + h))
    tiles, which = _dil_distance_tiles(s)
    tile_specs = [pl.BlockSpec(t.shape, lambda b, h: (0, 0, 0)) for t in tiles]
    return pl.pallas_call(
        functools.partial(_dil_kernel, s=s, which=tuple(which)),
        grid=(n_seq, N_HEADS),
        in_specs=[pl.BlockSpec(memory_space=pltpu.SMEM), blk(0), blk(N_HEADS), blk(2 * N_HEADS)] + tile_specs,
        out_specs=pl.BlockSpec((s, HEAD_DIM), lambda b, h: (b, h)),
        out_shape=jax.ShapeDtypeStruct((n_seq * s, D_MODEL), BF16),
        scratch_shapes=[pltpu.VMEM((s, HEAD_DIM), F32)] * 3,
        compiler_params=_params("parallel", "parallel"),
        name="dilated_attention",
    )(slopes, qkv, qkv, qkv, *[jnp.asarray(t) for t in tiles])


def _gqa_kernel(slope_ref, sink_ref, q_ref, k_ref, v_ref, o_ref, *, s, qb):
    kvh = pl.program_id(1)
    kb = qb + 2 * C_WINDOW

    def body(n, carry):
        q0 = pl.multiple_of(n * qb, qb)
        ks = pl.multiple_of(jnp.clip(n * qb - C_WINDOW, 0, s - kb), C_WINDOW)
        k = k_ref[pl.ds(ks, kb), :]
        v = v_ref[pl.ds(ks, kb), :]
        tq = q0 + lax.broadcasted_iota(jnp.int32, (qb, kb), 0)
        tk = ks + lax.broadcasted_iota(jnp.int32, (qb, kb), 1)
        rel = jnp.abs(tk - tq)
        valid = rel <= C_WINDOW
        relf = rel.astype(F32)
        for g in range(C_GROUP):
            head = kvh * C_GROUP + g
            slope = slope_ref[head]
            sink = sink_ref[head]
            q = q_ref[pl.ds(q0, qb), g * HEAD_DIM:(g + 1) * HEAD_DIM]
            sc = _qk_scores(q, k) * SCALE - slope * relf
            sc = jnp.where(valid, sc, NEG_INF)
            m = jnp.maximum(jnp.max(sc, axis=-1, keepdims=True), sink)
            e = jnp.exp(sc - m)
            l = jnp.sum(e, axis=-1, keepdims=True) + jnp.exp(sink - m)
            o = jnp.dot(e.astype(BF16), v, preferred_element_type=F32) / l
            o_ref[pl.ds(q0, qb), g * HEAD_DIM:(g + 1) * HEAD_DIM] = o.astype(o_ref.dtype)
        return carry

    lax.fori_loop(0, s // qb, body, 0)


def gqa_sink_attention(qkv, slopes, sink, n_seq, seq_off, s):
    gw = C_GROUP * HEAD_DIM
    kv0 = D_MODEL // HEAD_DIM
    smem = pl.BlockSpec(memory_space=pltpu.SMEM)
    return pl.pallas_call(
        functools.partial(_gqa_kernel, s=s, qb=256),
        grid=(n_seq, C_KV_HEADS),
        in_specs=[
            smem, smem,
            pl.BlockSpec((s, gw), lambda b, h: (seq_off + b, h)),
            pl.BlockSpec((s, HEAD_DIM), lambda b, h: (seq_off + b, kv0 + h)),
            pl.BlockSpec((s, HEAD_DIM), lambda b, h: (seq_off + b, kv0 + C_KV_HEADS + h)),
        ],
        out_specs=pl.BlockSpec((s, gw), lambda b, h: (b, h)),
        out_shape=jax.ShapeDtypeStruct((n_seq * s, D_MODEL), BF16),
        compiler_params=_params("parallel", "parallel"),
        name="gqa_sink_attention",
    )(slopes, sink, qkv, qkv, qkv)


def _pack_bf16_pairs(x):
    half = x.shape[1] // 2
    hi = lax.bitcast_convert_type(x[:, :half].astype(BF16).astype(F32), jnp.uint32)
    lo = lax.bitcast_convert_type(x[:, half:].astype(BF16).astype(F32), jnp.uint32)
    return hi | (lo >> 16)


def _unpack_bf16_pairs(w):
    hi = lax.bitcast_convert_type(w & jnp.uint32(0xFFFF0000), F32)
    lo = lax.bitcast_convert_type(w << 16, F32)
    return hi, lo


def _route_kernel(h_ref, g_ref, wh_ref, wl_ref, b_ref, xp_ref, r_ref):
    x = h_ref[...]
    ms = jnp.mean(x * x, axis=-1, keepdims=True)
    xn = x * lax.rsqrt(ms + NORM_EPS) * g_ref[...]
    xp_ref[...] = _pack_bf16_pairs(xn)
    xh = xn.astype(BF16)
    xl = (xn - xh.astype(F32)).astype(BF16)
    wh = wh_ref[...]
    logits = jnp.dot(xh, wh, preferred_element_type=F32) + (
        jnp.dot(xl, wh, preferred_element_type=F32) + jnp.dot(xh, wl_ref[...], preferred_element_type=F32))
    logits = logits + b_ref[...]
    lane = lax.broadcasted_iota(jnp.int32, logits.shape, 1).astype(F32)
    rmax = lambda a: jnp.max(a, axis=-1, keepdims=True)
    rmin = lambda a: jnp.min(a, axis=-1, keepdims=True)
    rsum = lambda a: jnp.sum(a, axis=-1, keepdims=True)

    gmask = lane < N_GROUPS
    gl = jnp.where(gmask, logits, NEG_INF)
    ge = jnp.where(gmask, jnp.exp(gl - rmax(gl)), 0.0)
    gp = jnp.where(gmask, ge / rsum(ge), -1.0)
    g_p = rmax(gp)
    g_idx = rmin(jnp.where(gp == g_p, lane, float(LANES)))

    lo = N_GROUPS + EXPERTS_PER_GROUP * g_idx
    emask = (lane >= lo) & (lane < lo + EXPERTS_PER_GROUP)
    el = jnp.where(emask, logits, NEG_INF)
    ee = jnp.where(emask, jnp.exp(el - rmax(el)), 0.0)
    ep = jnp.where(emask, ee / rsum(ee), -1.0)
    p1 = rmax(ep)
    i1 = rmin(jnp.where(ep == p1, lane, float(LANES)))
    ep2 = jnp.where(lane == i1, -1.0, ep)
    p2 = rmax(ep2)
    i2 = rmin(jnp.where(ep2 == p2, lane, float(LANES)))
    den = p1 + p2
    out = jnp.where(lane == 0, g_p * (p1 / den), 0.0)
    out = jnp.where(lane == 1, g_p * (p2 / den), out)
    out = jnp.where(lane == 2, i1 - N_GROUPS, out)
    out = jnp.where(lane == 3, i2 - N_GROUPS, out)
    r_ref[...] = out


def route_weights(w_group, b_group, w_expert, b_expert):
    n_pad = LANES - N_GROUPS - N_EXPERTS
    w = jnp.pad(jnp.concatenate([w_group, w_expert], axis=-1), ((0, 0), (0, 0), (0, n_pad)))
    b = jnp.pad(jnp.concatenate([b_group, b_expert], axis=-1), ((0, 0), (0, n_pad)))[:, None, :]
    wh = w.astype(BF16)
    wl = (w - wh.astype(F32)).astype(BF16)
    return wh, wl, b


def moe_route(h, g, wh, wl, b, layer, tm=256):
    t, d = h.shape
    return pl.pallas_call(
        _route_kernel,
        grid=(t // tm,),
        in_specs=[
            pl.BlockSpec((tm, d), lambda i: (i, 0)),
            pl.BlockSpec((1, d), lambda i: (0, 0)),
            _layer_spec((d, LANES), lambda i: (0, 0), layer),
            _layer_spec((d, LANES), lambda i: (0, 0), layer),
            _layer_spec((1, LANES), lambda i: (0, 0), layer),
        ],
        out_specs=[pl.BlockSpec((tm, d // 2), lambda i: (i, 0)), pl.BlockSpec((tm, LANES), lambda i: (i, 0))],
        out_shape=[jax.ShapeDtypeStruct((t, d // 2), jnp.uint32), jax.ShapeDtypeStruct((t, LANES), F32)],
        compiler_params=_params("parallel"),
        name="moe_route",
    )(h, g.reshape(1, d), wh, wl, b)


def moe_plan(expert_ids):
    t = expert_ids.shape[0]
    a = t * EXPERT_TOP_K
    n_blocks = a // MOE_BLOCK + N_EXPERTS
    cap = n_blocks * MOE_BLOCK
    flat_e = expert_ids.reshape(a)
    order = jnp.argsort(flat_e, stable=True).astype(jnp.int32)
    rank = jnp.argsort(order).astype(jnp.int32)
    counts = jnp.sum((flat_e[:, None] == jnp.arange(N_EXPERTS, dtype=jnp.int32)[None, :]).astype(jnp.int32), axis=0)
    padded = (counts + MOE_BLOCK - 1) // MOE_BLOCK * MOE_BLOCK
    start = jnp.cumsum(counts) - counts
    pend = jnp.cumsum(padded)
    pstart = pend - padded
    pos = (pstart[flat_e] + rank - start[flat_e]).astype(jnp.int32)
    block_first = jnp.arange(n_blocks, dtype=jnp.int32) * MOE_BLOCK
    block_e = jnp.minimum(jnp.sum((pend[None, :] <= block_first[:, None]).astype(jnp.int32), axis=1), N_EXPERTS - 1)
    slot = jnp.arange(cap, dtype=jnp.int32)
    slot_e = jnp.repeat(block_e, MOE_BLOCK)
    slot_rank = slot - pstart[slot_e]
    sorted_pos = jnp.clip(start[slot_e] + slot_rank, 0, a - 1)
    slot_tok = jnp.where(slot_rank < counts[slot_e], order[sorted_pos] // EXPERT_TOP_K, 0).astype(jnp.int32)
    n_used = (pend[N_EXPERTS - 1:] // MOE_BLOCK).astype(jnp.int32)
    return slot_tok, pos, block_e.astype(jnp.int32), n_used


def _row_copy(src_ref, dst_ref, sem, src_row, dst_row):
    return pltpu.make_async_copy(src_ref.at[pl.ds(src_row, 1), :], dst_ref.at[pl.ds(dst_row, 1), :], sem)


GATHER_UNROLL = 8


def _start_row_gather(src_hbm, idx_ref, idx0, idx_stride, dst, sem, inline=False):
    n = dst.shape[0]
    if inline:
        for r in range(n):
            _row_copy(src_hbm, dst, sem, idx_ref[idx0 + r * idx_stride], r).start()
        return

    def issue(it, carry):
        for u in range(GATHER_UNROLL):
            r = it * GATHER_UNROLL + u
            _row_copy(src_hbm, dst, sem, idx_ref[idx0 + r * idx_stride], r).start()
        return carry

    lax.fori_loop(0, n // GATHER_UNROLL, issue, 0)


def _wait_row_gather(src_hbm, dst, sem):
    pltpu.make_async_copy(src_hbm.at[pl.ds(0, dst.shape[0]), :], dst, sem).wait()


def _prefetched_gather(step, n_steps, start, wait):
    @pl.when(step == 0)
    def _():
        start(0, 0, False)

    slot = step % 2
    wait(slot)
    start(jnp.minimum(step + 1, n_steps - 1), 1 - slot, True)
    return slot


def _drain_gather(step, n_steps, wait):
    @pl.when(step == n_steps - 1)
    def _():
        wait(1 - step % 2)


def _moe_up_kernel(tok_ref, be_ref, nu_ref, x_hbm, wg_ref, wu_ref, o_ref, buf, sems):
    rb = pl.program_id(0)
    n_used = nu_ref[0]

    def start(block, slot, inline):
        _start_row_gather(x_hbm, tok_ref, block * MOE_BLOCK, 1, buf.at[slot], sems.at[slot], inline)

    def wait(slot):
        _wait_row_gather(x_hbm, buf.at[slot], sems.at[slot])

    @pl.when(rb < n_used)
    def _():
        slot = _prefetched_gather(rb, n_used, start, wait)
        x_hi, x_lo = _unpack_bf16_pairs(buf[slot])
        x_hi = x_hi.astype(BF16)
        x_lo = x_lo.astype(BF16)
        half = x_hi.shape[1]

        def proj(w_ref):
            return (jnp.dot(x_hi, w_ref[:half, :], preferred_element_type=F32)
                    + jnp.dot(x_lo, w_ref[half:, :], preferred_element_type=F32))

        g = proj(wg_ref)
        u = proj(wu_ref)
        o_ref[...] = (g * (1.0 / (1.0 + jnp.exp(-g))) * u).astype(o_ref.dtype)
        _drain_gather(rb, n_used, wait)

    @pl.when(rb >= n_used)
    def _():
        o_ref[...] = jnp.zeros_like(o_ref)


def moe_up(xp, slot_tok, w_gate, w_up, layer, block_e, n_used):
    d = w_gate.shape[2]
    n = w_gate.shape[3]
    cap = slot_tok.shape[0]
    w_spec = pl.BlockSpec((None, None, d, n), lambda rb, tok, be, nu: (layer, be[rb], 0, 0))
    return pl.pallas_call(
        _moe_up_kernel,
        grid_spec=pltpu.PrefetchScalarGridSpec(
            num_scalar_prefetch=3,
            grid=(cap // MOE_BLOCK,),
            in_specs=[pl.BlockSpec(memory_space=pl.ANY), w_spec, w_spec],
            out_specs=pl.BlockSpec((MOE_BLOCK, n), lambda rb, tok, be, nu: (rb, 0)),
            scratch_shapes=[pltpu.VMEM((2, MOE_BLOCK, d // 2), jnp.uint32), pltpu.SemaphoreType.DMA((2,))],
        ),
        out_shape=jax.ShapeDtypeStruct((cap, n), BF16),
        compiler_params=_params("arbitrary"),
        name="moe_up",
    )(slot_tok, block_e, n_used, xp, w_gate, w_up)


def _moe_down_kernel(be_ref, nu_ref, x_ref, w_ref, o_ref, wb_ref):
    rb = pl.program_id(0)
    n_used = nu_ref[0]

    @pl.when(rb < n_used)
    def _():
        @pl.when((rb == 0) | (be_ref[rb] != be_ref[jnp.maximum(rb - 1, 0)]))
        def _():
            wb_ref[...] = w_ref[...].astype(BF16)

        o_ref[...] = _pack_bf16_pairs(jnp.dot(x_ref[...], wb_ref[...], preferred_element_type=F32))

    @pl.when(rb >= n_used)
    def _():
        o_ref[...] = jnp.zeros_like(o_ref)


def moe_down(hs, w_down, layer, block_e, n_used):
    cap, k = hs.shape
    n = w_down.shape[3]
    return pl.pallas_call(
        _moe_down_kernel,
        grid_spec=pltpu.PrefetchScalarGridSpec(
            num_scalar_prefetch=2,
            grid=(cap // MOE_BLOCK,),
            in_specs=[
                pl.BlockSpec((MOE_BLOCK, k), lambda rb, be, nu: (rb, 0)),
                pl.BlockSpec((None, None, k, n), lambda rb, be, nu: (layer, be[rb], 0, 0)),
            ],
            out_specs=pl.BlockSpec((MOE_BLOCK, n // 2), lambda rb, be, nu: (rb, 0)),
            scratch_shapes=[pltpu.VMEM((k, n), BF16)],
        ),
        out_shape=jax.ShapeDtypeStruct((cap, n // 2), jnp.uint32),
        compiler_params=_params("arbitrary"),
        name="moe_down",
    )(block_e, n_used, hs, w_down)


def _combine_kernel(pos_ref, h_ref, g_ref, gain_ref, y_hbm, o_ref, xn_ref, buf, sems, *, tm):
    i = pl.program_id(0)
    n_steps = pl.num_programs(0)

    def start(block, slot, inline):
        for k in range(EXPERT_TOP_K):
            _start_row_gather(y_hbm, pos_ref, block * tm * EXPERT_TOP_K + k, EXPERT_TOP_K,
                              buf.at[slot, k], sems.at[slot, k], inline)

    def wait(slot):
        for k in range(EXPERT_TOP_K):
            _wait_row_gather(y_hbm, buf.at[slot, k], sems.at[slot, k])

    slot = _prefetched_gather(i, n_steps, start, wait)
    g = g_ref[...]
    half = h_ref.shape[1] // 2
    y0 = _unpack_bf16_pairs(buf[slot, 0])
    y1 = _unpack_bf16_pairs(buf[slot, 1])
    hn = [h_ref[:, c * half:(c + 1) * half] + (g[:, 0:1] * y0[c] + g[:, 1:2] * y1[c]) for c in range(2)]
    ssq = sum(jnp.sum(x * x, axis=-1, keepdims=True) for x in hn)
    r = lax.rsqrt(ssq / h_ref.shape[1] + NORM_EPS)
    for c in range(2):
        cols = slice(c * half, (c + 1) * half)
        o_ref[:, cols] = hn[c]
        xn_ref[:, cols] = (hn[c] * r * gain_ref[:, cols]).astype(xn_ref.dtype)
    _drain_gather(i, n_steps, wait)


def moe_combine(h, route, ys, pos, next_gain, tm=256):
    t, d = h.shape
    row = lambda i, pos: (i, 0)
    return pl.pallas_call(
        functools.partial(_combine_kernel, tm=tm),
        grid_spec=pltpu.PrefetchScalarGridSpec(
            num_scalar_prefetch=1,
            grid=(t // tm,),
            in_specs=[
                pl.BlockSpec((tm, d), row),
                pl.BlockSpec((tm, LANES), row),
                pl.BlockSpec((1, d), lambda i, pos: (0, 0)),
                pl.BlockSpec(memory_space=pl.ANY),
            ],
            out_specs=[pl.BlockSpec((tm, d), row), pl.BlockSpec((tm, d), row)],
            scratch_shapes=[pltpu.VMEM((2, EXPERT_TOP_K, tm, d // 2), jnp.uint32),
                            pltpu.SemaphoreType.DMA((2, EXPERT_TOP_K))],
        ),
        out_shape=[jax.ShapeDtypeStruct((t, d), F32), jax.ShapeDtypeStruct((t, d), BF16)],
        compiler_params=_params("arbitrary"),
        name="moe_combine",
    )(pos, h, route, next_gain.reshape(1, d), ys)


def hierarchical_moe_update(h, g, route_wh, route_wl, route_b, w_gate, w_up, w_down, layer, next_gain):
    xp, route = moe_route(h, g, route_wh, route_wl, route_b, layer)
    expert_ids = route[:, 2:4].astype(jnp.int32)
    slot_tok, pos, block_e, n_used = moe_plan(expert_ids)
    hs = moe_up(xp, slot_tok, w_gate, w_up, layer, block_e, n_used)
    ys = moe_down(hs, w_down, layer, block_e, n_used)
    return moe_combine(h, route, ys, pos, next_gain)


def kernel(x_prompt, x_sample, p_prompt, p_sample, norm_mix, norm_ffn, a_w_qkv, a_rpb, a_w_o, b_w_qkv, b_w_o,
           c_w_qkv, c_sink, c_w_o, moe_w_group, moe_b_group, moe_w_expert, moe_b_expert, moe_w_gate, moe_w_up,
           moe_w_down, ple_w_proj, ple_norm, ple_w_gate, final_norm):
    b1, s1, d = x_prompt.shape
    b2, s2, _ = x_sample.shape
    t1 = b1 * s1
    t2 = b2 * s2
    assert t1 % s2 == 0
    seqs = ((b1, 0, s1), (b2, t1 // s2, s2))

    h = jnp.concatenate([x_prompt.reshape(t1, d), x_sample.reshape(t2, d)], axis=0)
    p = jnp.concatenate([p_prompt.reshape(DEPTH, t1, D_PLE), p_sample.reshape(DEPTH, t2, D_PLE)], axis=1).astype(BF16)
    slopes = 2.0 ** (-8.0 * jnp.arange(1, N_HEADS + 1, dtype=F32) / N_HEADS)
    route_wh, route_wl, route_b = route_weights(moe_w_group, moe_b_group, moe_w_expert, moe_b_expert)

    w_qkv = [w.astype(BF16) for w in (a_w_qkv, b_w_qkv, c_w_qkv)]
    w_out = [w.astype(BF16) for w in (a_w_o, b_w_o, c_w_o)]
    w_gate = moe_w_gate.astype(BF16)
    w_up = moe_w_up.astype(BF16)
    w_ple_gate = ple_w_gate.astype(BF16)
    w_ple_proj = ple_w_proj.astype(BF16)

    for i in range(DEPTH):
        kind = i % N_MIXERS
        j = i // N_MIXERS
        xn = rmsnorm(h, norm_mix[i], BF16)
        if kind == 0:
            qkv = matmul(xn, w_qkv[kind], j, BF16)
            slabs = na_bias_slabs(a_rpb[j])
            o = [neighbourhood_attention(qkv, slabs, n, off, s) for n, off, s in seqs]
        elif kind == 1:
            qkv = matmul(xn, w_qkv[kind], j, F32)
            o = [dilated_attention(qkv, slopes, n, off, s) for n, off, s in seqs]
        else:
            qkv = matmul(xn, w_qkv[kind], j, BF16)
            o = [gqa_sink_attention(qkv, slopes, c_sink[j], n, off, s) for n, off, s in seqs]
        h = matmul_residual(o[0], o[1], w_out[kind], j, h)
        h, xn = hierarchical_moe_update(
            h, norm_ffn[i], route_wh, route_wl, route_b, w_gate, w_up, moe_w_down, i, ple_norm[i])
        h = ple_update(xn, w_ple_gate, p, w_ple_proj, i, h)

    y1 = rmsnorm(h, final_norm, F32, 0, t1)
    y2 = rmsnorm(h, final_norm, F32, t1, t2)
    return y1.reshape(b1, s1, d), y2.reshape(b2, s2, d)
```
